```python
import math
import jax
import jax.numpy as jnp
from jax import lax
import numpy as np

D_MODEL = 1024
BATCH = 8
SEQ = 2048
DEPTH = 4

CTX_LEN = 256
GRID_W = 64
EPS = 1e-6
GN_EPS = 1e-5
ROPE_BASE = 10000.0

RET_DK = 32
RET_DV = 64
RET_HEADS = (D_MODEL // 4) // RET_DV
RET_CHUNK = 128

GLA_DK = 32
GLA_DV = 64
GLA_HEADS = (D_MODEL // 4) // GLA_DV
GLA_CHUNK = 32
GLA_GATE_RANK = 16
GLA_GATE_TAU = 16.0

DIFF_HD = 64
DIFF_DV = 2 * DIFF_HD
DIFF_HEADS = (D_MODEL // 2) // DIFF_DV
Q_BLOCK = 128

MIX_WIDTH = RET_HEADS * RET_DV + GLA_HEADS * GLA_DV + DIFF_HEADS * DIFF_DV
D_FF = ((8 * D_MODEL + 3 * 256 - 1) // (3 * 256)) * 256

PROJ_SIZES = (
    RET_HEADS * RET_DK, RET_HEADS * RET_DK, RET_HEADS * RET_DV, RET_HEADS * RET_DV,
    GLA_HEADS * GLA_DK, GLA_HEADS * GLA_DK, GLA_HEADS * GLA_DV, GLA_HEADS * GLA_DV,
    2 * GLA_GATE_RANK,
    DIFF_HEADS * 2 * DIFF_HD, DIFF_HEADS * 2 * DIFF_HD, DIFF_HEADS * DIFF_DV,
)
PROJ_WIDTH = sum(PROJ_SIZES)

kernel_name = 'hybrid_ret_gla_diffattn_prefix_block'


def rms_norm(x, w):
    xf = x.astype(jnp.float32)
    y = xf * lax.rsqrt(jnp.mean(xf * xf, axis=-1, keepdims=True) + EPS)
    return (y * w.astype(jnp.float32)).astype(x.dtype)


def group_norm_heads(x):
    xf = x.astype(jnp.float32)
    mu = jnp.mean(xf, axis=-1, keepdims=True)
    var = jnp.mean(jnp.square(xf - mu), axis=-1, keepdims=True)
    return (xf - mu) * lax.rsqrt(var + GN_EPS)


def modulate(h, shift, scale):
    return h * (1.0 + scale) + shift


def angle_tables(pos, inv_freq):
    ang = pos.astype(jnp.float32)[:, None] * inv_freq[None, :]
    return jnp.cos(ang), jnp.sin(ang)


def axial_freqs(dim):
    half = dim // 2
    return 1.0 / (ROPE_BASE ** (jnp.arange(half, dtype=jnp.float32) / half))


def retnet_freqs(dim):
    return 1.0 / (ROPE_BASE ** jnp.linspace(0.0, 1.0, dim // 2, dtype=jnp.float32))


def apply_rope(x, cos, sin):
    x1, x2 = jnp.split(x, 2, axis=-1)
    return jnp.concatenate([x1 * cos - x2 * sin, x1 * sin + x2 * cos], axis=-1).astype(x.dtype)


def apply_axial_rope(x, tabs):
    cos_r, sin_r, cos_c, sin_c = tabs
    x_row, x_col = jnp.split(x, 2, axis=-1)
    return jnp.concatenate([apply_rope(x_row, cos_r, sin_r), apply_rope(x_col, cos_c, sin_c)], axis=-1)


def to_heads(t, n_heads):
    b, n, _ = t.shape
    return t.reshape(b, n, n_heads, -1).transpose(0, 2, 1, 3)


def from_heads(t):
    b, h, n, d = t.shape
    return t.transpose(0, 2, 1, 3).reshape(b, n, h * d)


def flip_seq(t):
    return jnp.flip(t, axis=2)


def split_proj(p):
    idx = [int(s) for s in np.cumsum(PROJ_SIZES)[:-1]]
    return jnp.split(p, idx, axis=-1)


def chunk_state_scan(decay, u, s0):
    def step(s, inp):
        d, du = inp
        return d[..., None] * s + du, s
    s_last, s_prev = lax.scan(step, s0, (jnp.moveaxis(decay, 2, 0), jnp.moveaxis(u, 2, 0)))
    return jnp.moveaxis(s_prev, 0, 2), s_last


def retention_chunked(q, k, v, log_gamma, s0):
    b, h, n_tok, dk = q.shape
    dv = v.shape[-1]
    n = n_tok // RET_CHUNK
    qc = q.astype(jnp.float32).reshape(b, h, n, RET_CHUNK, dk)
    kc = k.astype(jnp.float32).reshape(b, h, n, RET_CHUNK, dk)
    vc = v.astype(jnp.float32).reshape(b, h, n, RET_CHUNK, dv)
    i = jnp.arange(RET_CHUNK, dtype=jnp.float32)
    rel = i[:, None] - i[None, :]
    dmat = jnp.exp(jnp.where(rel >= 0, rel * log_gamma[:, None, None], -jnp.inf))
    scores = jnp.einsum('bhnid,bhnjd->bhnij', qc, kc) * dmat[:, None]
    intra = jnp.einsum('bhnij,bhnje->bhnie', scores, vc)
    q_dec = jnp.exp((i + 1.0)[None, :] * log_gamma[:, None])
    k_dec = jnp.exp((RET_CHUNK - 1.0 - i)[None, :] * log_gamma[:, None])
    u = jnp.einsum('bhnjd,bhnje->bhnde', kc * k_dec[None, :, None, :, None], vc)
    chunk_dec = jnp.broadcast_to(jnp.exp(RET_CHUNK * log_gamma)[None, :, None, None], (b, h, n, dk))
    s_prev, s_last = chunk_state_scan(chunk_dec, u, s0)
    inter = jnp.einsum('bhnid,bhnde->bhnie', qc * q_dec[None, :, None, :, None], s_prev)
    return (intra + inter).reshape(b, h, n_tok, dv), s_last


def gla_chunked(q, k, v, g, s0):
    b, h, n_tok, dk = q.shape
    dv = v.shape[-1]
    n = n_tok // GLA_CHUNK
    qc = q.astype(jnp.float32).reshape(b, h, n, GLA_CHUNK, dk)
    kc = k.astype(jnp.float32).reshape(b, h, n, GLA_CHUNK, dk)
    vc = v.astype(jnp.float32).reshape(b, h, n, GLA_CHUNK, dv)
    cum = lax.cumsum(g.astype(jnp.float32).reshape(b, h, n, GLA_CHUNK, dk), axis=3)
    i = jnp.arange(GLA_CHUNK)
    lower = (i[:, None] >= i[None, :])[..., None]
    pair = jnp.exp(jnp.where(lower, cum[:, :, :, :, None, :] - cum[:, :, :, None, :, :], -jnp.inf))
    attn = jnp.einsum('bhnid,bhnjd,bhnijd->bhnij', qc, kc, pair)
    intra = jnp.einsum('bhnij,bhnje->bhnie', attn, vc)
    cum_last = cum[:, :, :, -1:, :]
    u = jnp.einsum('bhnjd,bhnje->bhnde', kc * jnp.exp(cum_last - cum), vc)
    s_prev, s_last = chunk_state_scan(jnp.exp(cum_last[:, :, :, 0]), u, s0)
    inter = jnp.einsum('bhnid,bhnde->bhnie', qc * jnp.exp(cum), s_prev)
    return (intra + inter).reshape(b, h, n_tok, dv), s_last


def retention_bidir(q, k, v, log_gamma, s0_f, s0_b):
    o_f, s_f = retention_chunked(q, k, v, log_gamma[0], s0_f)
    o_b, s_b = retention_chunked(flip_seq(q), flip_seq(k), flip_seq(v), log_gamma[1], s0_b)
    return o_f + flip_seq(o_b), s_f, s_b


def gla_bidir(q, k, v, g_f, g_b, s0_f, s0_b):
    o_f, s_f = gla_chunked(q, k, v, g_f, s0_f)
    o_b, s_b = gla_chunked(flip_seq(q), flip_seq(k), flip_seq(v), flip_seq(g_b), s0_b)
    return o_f + flip_seq(o_b), s_f, s_b


def gla_gates(lr, w_gate, b_gate):
    lr_f, lr_b = jnp.split(lr, 2, axis=-1)
    z_f = (lr_f @ w_gate[0] + b_gate[0]).astype(jnp.float32)
    z_b = (lr_b @ w_gate[1] + b_gate[1]).astype(jnp.float32)
    g_f = to_heads(jax.nn.log_sigmoid(z_f) / GLA_GATE_TAU, GLA_HEADS)
    g_b = to_heads(jax.nn.log_sigmoid(z_b) / GLA_GATE_TAU, GLA_HEADS)
    return g_f, g_b


def diff_qk_heads(t):
    b, n, _ = t.shape
    return t.reshape(b, n, DIFF_HEADS, 2, DIFF_HD).transpose(0, 2, 3, 1, 4)


def diff_attend(q, k, v, lam):
    s = jnp.einsum('bhmqd,bhmkd->bhmqk', q.astype(jnp.float32), k.astype(jnp.float32)) * (DIFF_HD ** -0.5)
    p = jax.nn.softmax(s, axis=-1)
    a = p[:, :, 0] - lam * p[:, :, 1]
    return jnp.einsum('bhqk,bhkd->bhqd', a, v.astype(jnp.float32))


def token_mixers(p_lat, p_ctx, ret_rot, axial_rot, ret_decay_logit, gla_w_gate, gla_b_gate,
                 gla_norm_w, diff_lambda, diff_norm_w, lam_init, ctx_out):
    rq, rk, rv, rg, gq, gk, gv, gr, glr, dq, dk, dv = split_proj(p_lat)
    rq_c, rk_c, rv_c, rg_c, gq_c, gk_c, gv_c, gr_c, glr_c, dq_c, dk_c, dv_c = split_proj(p_ctx)
    bsz = p_lat.shape[0]

    log_gamma = jax.nn.log_sigmoid(ret_decay_logit.astype(jnp.float32))
    k_scale = RET_DK ** -0.5
    zero_r = jnp.zeros((bsz, RET_HEADS, RET_DK, RET_DV), jnp.float32)
    o_rc, s_rf, s_rb = retention_bidir(to_heads(rq_c, RET_HEADS), to_heads(rk_c, RET_HEADS) * k_scale,
                                       to_heads(rv_c, RET_HEADS), log_gamma, zero_r, zero_r)
    cos_t, sin_t = ret_rot
    o_r, _, _ = retention_bidir(apply_rope(to_heads(rq, RET_HEADS), cos_t, sin_t),
                                apply_rope(to_heads(rk, RET_HEADS), cos_t, sin_t) * k_scale,
                                to_heads(rv, RET_HEADS), log_gamma, s_rf, s_rb)
    ret_out = jax.nn.silu(rg) * from_heads(group_norm_heads(o_r))

    q_scale = GLA_DK ** -0.5
    zero_g = jnp.zeros((bsz, GLA_HEADS, GLA_DK, GLA_DV), jnp.float32)
    gf_c, gb_c = gla_gates(glr_c, gla_w_gate, gla_b_gate)
    o_gc, s_gf, s_gb = gla_bidir(to_heads(gq_c, GLA_HEADS) * q_scale, to_heads(gk_c, GLA_HEADS),
                                 to_heads(gv_c, GLA_HEADS), gf_c, gb_c, zero_g, zero_g)
    g_f, g_b = gla_gates(glr, gla_w_gate, gla_b_gate)
    o_g, _, _ = gla_bidir(to_heads(gq, GLA_HEADS) * q_scale, to_heads(gk, GLA_HEADS),
                          to_heads(gv, GLA_HEADS), g_f, g_b, s_gf, s_gb)
    gla_out = jax.nn.silu(gr) * from_heads(rms_norm(o_g, gla_norm_w))

    lam = (jnp.exp(jnp.sum(diff_lambda[0] * diff_lambda[1])) - jnp.exp(jnp.sum(diff_lambda[2] * diff_lambda[3]))
           + lam_init).astype(jnp.float32)
    d_q = apply_axial_rope(diff_qk_heads(dq), axial_rot)
    d_k = apply_axial_rope(diff_qk_heads(dk), axial_rot)
    d_k_c = diff_qk_heads(dk_c)
    d_v_c = to_heads(dv_c, DIFF_HEADS)
    k_all = jnp.concatenate([d_k, d_k_c], axis=3)
    v_all = jnp.concatenate([to_heads(dv, DIFF_HEADS), d_v_c], axis=2)
    n_tok = d_q.shape[3]
    q_blocks = jnp.moveaxis(d_q.reshape(bsz, DIFF_HEADS, 2, n_tok // Q_BLOCK, Q_BLOCK, DIFF_HD), 3, 0)
    o_blocks = lax.map(lambda qb: diff_attend(qb, k_all, v_all, lam), q_blocks)
    o_d = jnp.moveaxis(o_blocks, 0, 2).reshape(bsz, DIFF_HEADS, n_tok, DIFF_DV)
    diff_out = from_heads(rms_norm(o_d, diff_norm_w) * (1.0 - lam_init))

    o_lat = jnp.concatenate([ret_out, gla_out, diff_out], axis=-1)
    if not ctx_out:
        return o_lat, None
    o_dc = diff_attend(diff_qk_heads(dq_c), d_k_c, d_v_c, lam)
    o_ctx = jnp.concatenate([
        jax.nn.silu(rg_c) * from_heads(group_norm_heads(o_rc)),
        jax.nn.silu(gr_c) * from_heads(rms_norm(o_gc, gla_norm_w)),
        from_heads(rms_norm(o_dc, diff_norm_w) * (1.0 - lam_init)),
    ], axis=-1)
    return o_lat, o_ctx


def swiglu(h, w_in, w_out):
    gate, up = jnp.split(h @ w_in, 2, axis=-1)
    return (jax.nn.silu(gate) * up) @ w_out


def setup_inputs(seed: int = 0) -> dict:
    key = jax.random.key(seed)
    ks = jax.random.split(key, 20)

    def nrm(k, shape, scale):
        return jax.random.normal(k, shape, jnp.float32) * scale

    ret_logit0 = np.log(2.0 ** (5.0 + np.arange(RET_HEADS)) - 1.0).astype(np.float32)
    return {
        'x': nrm(ks[0], (BATCH, SEQ, D_MODEL), 1.0),
        'c': nrm(ks[1], (BATCH, D_MODEL), 1.0),
        'ctx': nrm(ks[2], (BATCH, CTX_LEN, D_MODEL), 1.0),
        'c_ctx': nrm(ks[3], (D_MODEL,), 1.0),
        'w_ada': nrm(ks[4], (DEPTH, D_MODEL, 6 * D_MODEL), 0.5 * D_MODEL ** -0.5),
        'b_ada': nrm(ks[5], (DEPTH, 6 * D_MODEL), 0.02),
        'norm1_w': 1.0 + nrm(ks[6], (DEPTH, D_MODEL), 0.02),
        'w_in': nrm(ks[7], (DEPTH, D_MODEL, PROJ_WIDTH), D_MODEL ** -0.5),
        'ret_decay_logit': jnp.asarray(ret_logit0) + nrm(ks[8], (DEPTH, 2, RET_HEADS), 0.05),
        'gla_w_gate': nrm(ks[9], (DEPTH, 2, GLA_GATE_RANK, GLA_HEADS * GLA_DK), GLA_GATE_RANK ** -0.5),
        'gla_b_gate': nrm(ks[10], (DEPTH, 2, GLA_HEADS * GLA_DK), 0.1),
        'gla_norm_w': 1.0 + nrm(ks[11], (DEPTH, GLA_DV), 0.02),
        'diff_lambda': nrm(ks[12], (DEPTH, 4, DIFF_HD), 0.1),
        'diff_norm_w': 1.0 + nrm(ks[13], (DEPTH, DIFF_DV), 0.02),
        'w_out': nrm(ks[14], (DEPTH, MIX_WIDTH, D_MODEL), MIX_WIDTH ** -0.5),
        'norm2_w': 1.0 + nrm(ks[15], (DEPTH, D_MODEL), 0.02),
        'w_ffn_in': nrm(ks[16], (DEPTH, D_MODEL, 2 * D_FF), D_MODEL ** -0.5),
        'w_ffn_out': nrm(ks[17], (DEPTH, D_FF, D_MODEL), D_FF ** -0.5),
        'final_norm_w': 1.0 + nrm(ks[18], (D_MODEL,), 0.02),
    }


def reference(x, c, ctx, c_ctx, w_ada, b_ada, norm1_w, w_in, ret_decay_logit, gla_w_gate,
              gla_b_gate, gla_norm_w, diff_lambda, diff_norm_w, w_out, norm2_w, w_ffn_in,
              w_ffn_out, final_norm_w):
    n_lat = x.shape[1]
    rows = n_lat // GRID_W
    row_idx, col_idx = jnp.meshgrid(jnp.arange(rows), jnp.arange(GRID_W), indexing='ij')
    axial_rot = (*angle_tables(row_idx.reshape(-1), axial_freqs(DIFF_HD // 2)),
                 *angle_tables(col_idx.reshape(-1), axial_freqs(DIFF_HD // 2)))
    ret_rot = angle_tables(jnp.arange(n_lat), retnet_freqs(RET_DK))
    cond_lat = jax.nn.silu(c)[:, None, :]
    cond_ctx = jax.nn.silu(c_ctx)[None, None, :]

    for layer in range(DEPTH):
        last = layer == DEPTH - 1
        lam_init = 0.8 - 0.6 * math.exp(-0.3 * layer)
        m = jnp.split(cond_lat @ w_ada[layer] + b_ada[layer], 6, axis=-1)
        mc = jnp.split(cond_ctx @ w_ada[layer] + b_ada[layer], 6, axis=-1)

        h = modulate(rms_norm(x, norm1_w[layer]), m[0], m[1])
        h_c = modulate(rms_norm(ctx, norm1_w[layer]), mc[0], mc[1])
        o, o_c = token_mixers(h @ w_in[layer], h_c @ w_in[layer], ret_rot, axial_rot,
                              ret_decay_logit[layer], gla_w_gate[layer], gla_b_gate[layer],
                              gla_norm_w[layer], diff_lambda[layer], diff_norm_w[layer],
                              lam_init, not last)
        x = x + m[2] * (o @ w_out[layer])
        x = x + m[5] * swiglu(modulate(rms_norm(x, norm2_w[layer]), m[3], m[4]),
                              w_ffn_in[layer], w_ffn_out[layer])
        if not last:
            ctx = ctx + mc[2] * (o_c @ w_out[layer])
            ctx = ctx + mc[5] * swiglu(modulate(rms_norm(ctx, norm2_w[layer]), mc[3], mc[4]),
                                       w_ffn_in[layer], w_ffn_out[layer])

    return rms_norm(x, final_norm_w)
```

```python
import functools
import math

import jax
import jax.numpy as jnp
import numpy as np
from jax import lax
from jax.experimental import pallas as pl
from jax.experimental.pallas import tpu as pltpu

F32 = jnp.float32
BF16 = jnp.bfloat16

D = 1024
DEPTH = 4
CTX = 256
GRID_W = 64
EPS = 1e-6
GN_EPS = 1e-5
ROPE_BASE = 10000.0

NH = 4
RET_DK, RET_DV = 32, 64
GLA_DK, GLA_DV = 32, 64
GLA_RANK = 16
GLA_TAU = 16.0
DIFF_HD, DIFF_DV = 64, 128
D_FF = 2816

O_RQ, O_RK, O_RV, O_RG = 0, 128, 256, 512
O_GQ, O_GK, O_GV, O_GR = 768, 896, 1024, 1280
O_DQ, O_DK, O_DV, O_LR = 1536, 2048, 2560, 3072
PROJ_PAD = 3200

LANE = 128
TM = 256
TQ = 256
RET_C = 128
GLA_C = 64
VMEM_LIMIT = 56 * 1024 * 1024


def _mm(a, b):
    return jnp.dot(a, b, preferred_element_type=F32)


def _mm_nt(a, b):
    return lax.dot_general(a, b, (((1,), (1,)), ((), ())), preferred_element_type=F32)


def _mm_tn(a, b):
    return lax.dot_general(a, b, (((0,), (0,)), ((), ())), preferred_element_type=F32)


def _split(x):
    hi = x.astype(BF16)
    lo = (x - hi.astype(F32)).astype(BF16)
    return hi, lo


def _mm_hilo(x, m_bf16):
    hi, lo = _split(x)
    return _mm(hi, m_bf16) + _mm(lo, m_bf16)


def _hilo_mm(m_bf16, x):
    hi, lo = _split(x)
    return _mm(m_bf16, hi) + _mm(m_bf16, lo)


def _log_sigmoid(z):
    return jnp.minimum(z, 0.0) - jnp.log(1.0 + jnp.exp(-jnp.abs(z)))


def _silu(z):
    return z / (1.0 + jnp.exp(-z))


def _iota(shape, dim):
    return lax.broadcasted_iota(jnp.int32, shape, dim)


def _group_avg_matrix(n, group_shift):
    r = _iota((n, n), 0) >> group_shift
    c = _iota((n, n), 1) >> group_shift
    return jnp.where(r == c, 1.0 / (1 << group_shift), 0.0).astype(BF16)


def _ada_kernel(cond_ref, w_ref, b_ref, o_ref):
    cond = _silu(cond_ref[...])
    w = w_ref[0]
    c_hi, c_lo = _split(cond)
    w_hi, w_lo = _split(w)
    o_ref[0] = _mm(c_hi, w_hi) + _mm(c_hi, w_lo) + _mm(c_lo, w_hi) + b_ref[0]


def _ada(cond, w_ada, b_ada):
    bn = 512
    n = w_ada.shape[-1]
    return pl.pallas_call(
        _ada_kernel,
        grid=(DEPTH, n // bn),
        in_specs=[
            pl.BlockSpec((16, D), lambda l, j: (0, 0)),
            pl.BlockSpec((1, D, bn), lambda l, j: (l, 0, j)),
            pl.BlockSpec((1, 1, bn), lambda l, j: (l, 0, j)),
        ],
        out_specs=pl.BlockSpec((1, 16, bn), lambda l, j: (l, 0, j)),
        out_shape=jax.ShapeDtypeStruct((DEPTH, 16, n), F32),
        compiler_params=pltpu.CompilerParams(dimension_semantics=("arbitrary", "arbitrary")),
        name="ada",
    )(cond, w_ada, b_ada.reshape(DEPTH, 1, n))


def _mod_spec(layer, j, batch):
    def index(b, t):
        row = jnp.where(t < CTX // TM, batch, b)
        return ((layer * 16 + row) * 6 + j, 0, 0)
    return pl.BlockSpec((None, 1, D), index)


def _rope(x, cos, sa, sb):
    outs = []
    for j in range(x.shape[1] // LANE):
        blk = x[:, j * LANE:(j + 1) * LANE]
        fwd = pltpu.roll(blk, LANE - 16, 1)
        bwd = pltpu.roll(blk, 16, 1)
        outs.append(blk * cos + fwd * sa + bwd * sb)
    return outs[0] if len(outs) == 1 else jnp.concatenate(outs, axis=1)


def _proj_kernel(x_ref, shift_ref, scale_ref, nw_ref, w_ref,
                 rcos, rsa, rsb, dcos, dsa, dsb,
                 rq_o, rk_o, rv_o, rg_o, gq_o, gk_o, gv_o, gr_o, dq_o, dk_o, dv_o, lr_o):
    x = x_ref[0]
    y = x * lax.rsqrt(jnp.mean(x * x, axis=-1, keepdims=True) + EPS) * nw_ref[...]
    h = y * (1.0 + scale_ref[...]) + shift_ref[...]
    p = _mm(h.astype(BF16), w_ref[0])

    rc, ra, rb = rcos[...], rsa[...], rsb[...]
    rq_o[0] = _rope(p[:, O_RQ:O_RQ + 128], rc, ra, rb).astype(BF16)
    rk_o[0] = (_rope(p[:, O_RK:O_RK + 128], rc, ra, rb) * (RET_DK ** -0.5)).astype(BF16)
    rv_o[0] = p[:, O_RV:O_RV + 256].astype(BF16)
    rg_o[0] = p[:, O_RG:O_RG + 256]
    gq_o[0] = (p[:, O_GQ:O_GQ + 128] * (GLA_DK ** -0.5)).astype(BF16)
    gk_o[0] = p[:, O_GK:O_GK + 128].astype(BF16)
    gv_o[0] = p[:, O_GV:O_GV + 256].astype(BF16)
    gr_o[0] = p[:, O_GR:O_GR + 256]
    dc, da, db = dcos[...], dsa[...], dsb[...]
    dq_o[0] = (_rope(p[:, O_DQ:O_DQ + 512], dc, da, db) * (DIFF_HD ** -0.5)).astype(BF16)
    dk_o[0] = _rope(p[:, O_DK:O_DK + 512], dc, da, db).astype(BF16)
    dv_o[0] = p[:, O_DV:O_DV + 512].astype(BF16)
    lr_o[0] = p[:, O_LR:O_LR + 128].astype(BF16)


def _proj(layer, batch, xs, mods, norm1_w, w_in_r, tabs):
    t_all = xs.shape[1]
    tok = lambda n: pl.BlockSpec((1, TM, n), lambda b, t: (b, t, 0))
    tab = pl.BlockSpec((TM, LANE), lambda b, t: (t, 0))
    widths = [128, 128, 256, 256, 128, 128, 256, 256, 512, 512, 512, 128]
    dtypes = [BF16, BF16, BF16, F32, BF16, BF16, BF16, F32, BF16, BF16, BF16, BF16]
    return pl.pallas_call(
        _proj_kernel,
        grid=(batch, t_all // TM),
        in_specs=[
            tok(D),
            _mod_spec(layer, 0, batch), _mod_spec(layer, 1, batch),
            pl.BlockSpec((None, 1, D), lambda b, t: (layer, 0, 0)),
            pl.BlockSpec((1, D, PROJ_PAD), lambda b, t: (layer, 0, 0), pipeline_mode=pl.Buffered(1)),
            tab, tab, tab, tab, tab, tab,
        ],
        out_specs=[tok(n) for n in widths],
        out_shape=[jax.ShapeDtypeStruct((batch, t_all, n), dt) for n, dt in zip(widths, dtypes)],
        compiler_params=pltpu.CompilerParams(
            dimension_semantics=("arbitrary", "arbitrary"), vmem_limit_bytes=VMEM_LIMIT),
        name="proj",
    )(xs, mods, mods, norm1_w, w_in_r, *tabs)


def _ret_kernel(logit_ref, q_ref, k_ref, v_ref, g_ref, o_ref, ut_f, ut_b, s_f, s_b):
    n_chunks = q_ref.shape[1] // RET_C
    n_ctx = CTX // RET_C
    c = RET_C

    lane_head = _iota((c, LANE), 1) >> 5
    i_f = _iota((c, LANE), 0).astype(F32)

    def lanes_of(d):
        z = jnp.zeros((c, LANE), F32)
        for h in range(NH):
            z = jnp.where(lane_head == h, logit_ref[d, h], z)
        return _log_sigmoid(z)

    lg_f, lg_b = lanes_of(0), lanes_of(1)
    kdec_f = jnp.exp((c - 1.0 - i_f) * lg_f)
    kdec_b = jnp.exp(i_f * lg_b)
    qdec_f = jnp.exp((i_f + 1.0) * lg_f)
    qdec_b = jnp.exp((c - i_f) * lg_b)
    cd_f = jnp.exp(c * lg_f[0:1])
    cd_b = jnp.exp(c * lg_b[0:1])

    bd = (_iota((NH * RET_DV, LANE), 0) >> 6) == (_iota((NH * RET_DV, LANE), 1) >> 5)

    def p0(n, carry):
        rows = pl.ds(pl.multiple_of(n * c, c), c)
        k = k_ref[0, rows, :].astype(F32)
        v = v_ref[0, rows, :]
        ut_f[n] = jnp.where(bd, _mm_tn(v, (k * kdec_f).astype(BF16)), 0.0)
        ut_b[n] = jnp.where(bd, _mm_tn(v, (k * kdec_b).astype(BF16)), 0.0)
        return carry

    lax.fori_loop(0, n_chunks, p0, 0)

    s = jnp.zeros((NH * RET_DV, LANE), F32)
    for n in range(n_chunks):
        s_f[n] = s.astype(BF16)
        s = s * cd_f + ut_f[n]
    s = jnp.zeros((NH * RET_DV, LANE), F32)
    for n in list(range(n_ctx - 1, -1, -1)) + list(range(n_chunks - 1, n_ctx - 1, -1)):
        s_b[n] = s.astype(BF16)
        s = s * cd_b + ut_b[n]

    rel = (_iota((c, c), 0) - _iota((c, c), 1)).astype(F32)
    eye = jnp.where(rel == 0.0, 1.0, 0.0)
    dmats = []
    for h in range(NH):
        lf = _log_sigmoid(jnp.full((c, c), logit_ref[0, h], F32))
        lb = _log_sigmoid(jnp.full((c, c), logit_ref[1, h], F32))
        dmats.append(jnp.exp(jnp.where(rel >= 0.0, rel * lf, -rel * lb)) + eye)
    vhead = _iota((c, NH * RET_DV), 1) >> 6
    avg = _group_avg_matrix(NH * RET_DV, 6)

    def p2(n, carry):
        rows = pl.ds(pl.multiple_of(n * c, c), c)
        q = q_ref[0, rows, :].astype(F32)
        k = k_ref[0, rows, :]
        v = v_ref[0, rows, :]
        o = _mm_nt((q * qdec_f).astype(BF16), s_f[n]) + _mm_nt((q * qdec_b).astype(BF16), s_b[n])
        for h in range(NH):
            qh = jnp.where(lane_head == h, q, 0.0).astype(BF16)
            sc = (_mm_nt(qh, k) * dmats[h]).astype(BF16)
            o = o + jnp.where(vhead == h, _mm(sc, v), 0.0)
        mu = _mm_hilo(o, avg)
        dlt = o - mu
        var = _mm_hilo(dlt * dlt, avg)
        o_ref[0, rows, :] = (_silu(g_ref[0, rows, :]) * dlt * lax.rsqrt(var + GN_EPS)).astype(BF16)
        return carry

    lax.fori_loop(0, n_chunks, p2, 0)


def _ret(layer, q, k, v, g, ret_decay_logit):
    batch, t_all, _ = q.shape
    n_chunks = t_all // RET_C
    tok = lambda n: pl.BlockSpec((1, t_all, n), lambda b, lg: (b, 0, 0))
    return pl.pallas_call(
        _ret_kernel,
        grid_spec=pltpu.PrefetchScalarGridSpec(
            num_scalar_prefetch=1,
            grid=(batch,),
            in_specs=[tok(128), tok(128), tok(256), tok(256)],
            out_specs=tok(256),
            scratch_shapes=[
                pltpu.VMEM((n_chunks, NH * RET_DV, LANE), F32),
                pltpu.VMEM((n_chunks, NH * RET_DV, LANE), F32),
                pltpu.VMEM((n_chunks, NH * RET_DV, LANE), BF16),
                pltpu.VMEM((n_chunks, NH * RET_DV, LANE), BF16),
            ],
        ),
        out_shape=jax.ShapeDtypeStruct((batch, t_all, 256), BF16),
        compiler_params=pltpu.CompilerParams(
            dimension_semantics=("arbitrary",), vmem_limit_bytes=VMEM_LIMIT),
        name="ret",
    )(ret_decay_logit[layer], q, k, v, g)


def _gla_kernel(q_ref, k_ref, v_ref, lr_ref, g_ref, wgf_ref, wgb_ref, bg_ref, nw_ref, o_ref,
                cf_s, cb_s, ut_f, ut_b, dec_f, dec_b, s_f, s_b):
    c = GLA_C
    n_chunks = q_ref.shape[1] // c
    n_ctx = CTX // c
    nd = NH * GLA_DV

    ltri = jnp.where(_iota((c, c), 0) >= _iota((c, c), 1), 1.0, 0.0)
    utri = jnp.where(_iota((c, c), 0) <= _iota((c, c), 1), 1.0, 0.0)
    ltri_b, utri_b = ltri.astype(BF16), utri.astype(BF16)
    bd = (_iota((nd, LANE), 0) >> 6) == (_iota((nd, LANE), 1) >> 5)
    wgf, wgb = wgf_ref[...], wgb_ref[...]
    bgf, bgb = bg_ref[0:1, :], bg_ref[1:2, :]

    def p0(n, carry):
        rows = pl.ds(pl.multiple_of(n * c, c), c)
        lr = lr_ref[0, rows, :]
        g_fw = _log_sigmoid(_mm(lr, wgf) + bgf) * (1.0 / GLA_TAU)
        g_bw = _log_sigmoid(_mm(lr, wgb) + bgb) * (1.0 / GLA_TAU)
        cf = _hilo_mm(ltri_b, g_fw)
        cb = _hilo_mm(utri_b, g_bw)
        cf_s[rows, :] = cf
        cb_s[rows, :] = cb
        cf_end, cb_end = cf[c - 1:c, :], cb[0:1, :]
        k = k_ref[0, rows, :].astype(F32)
        v = v_ref[0, rows, :]
        ut_f[n] = jnp.where(bd, _mm_tn(v, (k * jnp.exp(cf_end - cf)).astype(BF16)), 0.0)
        ut_b[n] = jnp.where(bd, _mm_tn(v, (k * jnp.exp(cb_end - cb)).astype(BF16)), 0.0)
        dec_f[n] = jnp.broadcast_to(jnp.exp(cf_end), (8, LANE))
        dec_b[n] = jnp.broadcast_to(jnp.exp(cb_end), (8, LANE))
        return carry

    lax.fori_loop(0, n_chunks, p0, 0)

    def scan_f(n, s):
        s_f[n] = s.astype(BF16)
        return s * dec_f[n][0:1, :] + ut_f[n]

    def scan_b(i, s, hi):
        n = hi - 1 - i
        s_b[n] = s.astype(BF16)
        return s * dec_b[n][0:1, :] + ut_b[n]

    zero = jnp.zeros((nd, LANE), F32)
    lax.fori_loop(0, n_chunks, scan_f, zero)
    s_ctx = lax.fori_loop(0, n_ctx, functools.partial(scan_b, hi=n_ctx), zero)
    lax.fori_loop(0, n_chunks - n_ctx, functools.partial(scan_b, hi=n_chunks), s_ctx)

    lane_head = _iota((c, LANE), 1) >> 5
    vhead = _iota((c, nd), 1) >> 6
    avg = _group_avg_matrix(nd, 6)
    nw = nw_ref[...]

    def p2(n, carry):
        rows = pl.ds(pl.multiple_of(n * c, c), c)
        q = q_ref[0, rows, :].astype(F32)
        k = k_ref[0, rows, :].astype(F32)
        v = v_ref[0, rows, :]
        cf, cb = cf_s[rows, :], cb_s[rows, :]
        o = (_mm_nt((q * jnp.exp(cf)).astype(BF16), s_f[n])
             + _mm_nt((q * jnp.exp(cb)).astype(BF16), s_b[n]))
        rf, rb = cf[c // 2 - 1:c // 2, :], cb[c // 2:c // 2 + 1, :]
        qf, kf = q * jnp.exp(cf - rf), (k * jnp.exp(rf - cf)).astype(BF16)
        qb, kb = q * jnp.exp(cb - rb), (k * jnp.exp(rb - cb)).astype(BF16)
        for h in range(NH):
            a_f = _mm_nt(jnp.where(lane_head == h, qf, 0.0).astype(BF16), kf)
            a_b = _mm_nt(jnp.where(lane_head == h, qb, 0.0).astype(BF16), kb)
            a = (a_f * ltri + a_b * utri).astype(BF16)
            o = o + jnp.where(vhead == h, _mm(a, v), 0.0)
        ms = _mm_hilo(o * o, avg)
        y = o * lax.rsqrt(ms + EPS) * nw
        o_ref[0, rows, :] = (_silu(g_ref[0, rows, :]) * y).astype(BF16)
        return carry

    lax.fori_loop(0, n_chunks, p2, 0)


def _gla(layer, q, k, v, lr, g, wgf, wgb, bg, nw):
    batch, t_all, _ = q.shape
    n_chunks = t_all // GLA_C
    nd = NH * GLA_DV
    tok = lambda n: pl.BlockSpec((1, t_all, n), lambda b: (b, 0, 0))
    par = lambda r, n: pl.BlockSpec((None, r, n), lambda b: (layer, 0, 0))
    return pl.pallas_call(
        _gla_kernel,
        grid=(batch,),
        in_specs=[tok(128), tok(128), tok(256), tok(128), tok(256),
                  par(128, 128), par(128, 128), par(2, 128), par(1, 256)],
        out_specs=tok(256),
        out_shape=jax.ShapeDtypeStruct((batch, t_all, 256), BF16),
        scratch_shapes=[
            pltpu.VMEM((t_all, LANE), F32), pltpu.VMEM((t_all, LANE), F32),
            pltpu.VMEM((n_chunks, nd, LANE), F32), pltpu.VMEM((n_chunks, nd, LANE), F32),
            pltpu.VMEM((n_chunks, 8, LANE), F32), pltpu.VMEM((n_chunks, 8, LANE), F32),
            pltpu.VMEM((n_chunks, nd, LANE), BF16), pltpu.VMEM((n_chunks, nd, LANE), BF16),
        ],
        compiler_params=pltpu.CompilerParams(
            dimension_semantics=("arbitrary",), vmem_limit_bytes=VMEM_LIMIT),
        name="gla",
    )(q, k, v, lr, g, wgf, wgb, bg, nw)


def _attn_kernel(q_ref, k_ref, v_ref, lam_ref, nw_ref, o_ref, *, lam_init):
    qi = pl.program_id(2)
    q = q_ref[0]
    first = _iota(q.shape, 1) < DIFF_HD
    q1 = jnp.where(first, q, jnp.zeros_like(q))
    q2 = jnp.where(first, jnp.zeros_like(q), q)
    lp = lam_ref[...]
    lam = (jnp.exp(jnp.sum(lp[0:1] * lp[1:2], axis=-1, keepdims=True))
           - jnp.exp(jnp.sum(lp[2:3] * lp[3:4], axis=-1, keepdims=True)) + lam_init)

    def soft_pv(qh, k, v):
        s = _mm_nt(qh, k)
        e = jnp.exp(s - jnp.max(s, axis=-1, keepdims=True))
        return _mm(e.astype(BF16), v) / jnp.sum(e, axis=-1, keepdims=True)

    def attend(n_keys):
        k = k_ref[0, 0:n_keys, :]
        v = v_ref[0, 0:n_keys, :]
        o = soft_pv(q1, k, v) - lam * soft_pv(q2, k, v)
        y = o * lax.rsqrt(jnp.mean(o * o, axis=-1, keepdims=True) + EPS) * nw_ref[...]
        o_ref[0] = (y * (1.0 - lam_init)).astype(BF16)

    @pl.when(qi == 0)
    def _():
        attend(CTX)

    @pl.when(qi > 0)
    def _():
        attend(k_ref.shape[1])


def _attn(layer, dq, dk, dv, diff_lambda, diff_norm_w):
    batch, t_all, _ = dq.shape
    lam_init = 0.8 - 0.6 * math.exp(-0.3 * layer)
    return pl.pallas_call(
        functools.partial(_attn_kernel, lam_init=lam_init),
        grid=(batch, NH, t_all // TQ),
        in_specs=[
            pl.BlockSpec((1, TQ, LANE), lambda b, h, i: (b, i, h)),
            pl.BlockSpec((1, t_all, LANE), lambda b, h, i: (b, 0, h)),
            pl.BlockSpec((1, t_all, LANE), lambda b, h, i: (b, 0, h)),
            pl.BlockSpec((None, 4, DIFF_HD), lambda b, h, i: (layer, 0, 0)),
            pl.BlockSpec((None, 1, DIFF_DV), lambda b, h, i: (layer, 0, 0)),
        ],
        out_specs=pl.BlockSpec((1, TQ, LANE), lambda b, h, i: (b, i, h)),
        out_shape=jax.ShapeDtypeStruct((batch, t_all, NH * DIFF_DV), BF16),
        compiler_params=pltpu.CompilerParams(
            dimension_semantics=("arbitrary", "arbitrary", "arbitrary"), vmem_limit_bytes=VMEM_LIMIT),
        name="attn",
    )(dq, dk, dv, diff_lambda, diff_norm_w)


def _post_kernel(x_ref, ro_ref, go_ref, do_ref, g1_ref, sh_ref, sc_ref, g2_ref,
                 wo_ref, nw_ref, wi_ref, wf_ref, fw_ref, o_ref, *, final):
    o = jnp.concatenate([ro_ref[0], go_ref[0], do_ref[0]], axis=-1)
    x = x_ref[0] + g1_ref[...] * _mm(o, wo_ref[0])
    y = x * lax.rsqrt(jnp.mean(x * x, axis=-1, keepdims=True) + EPS) * nw_ref[...]
    h = (y * (1.0 + sc_ref[...]) + sh_ref[...]).astype(BF16)
    gu = _mm(h, wi_ref[0])
    act = (_silu(gu[:, :D_FF]) * gu[:, D_FF:]).astype(BF16)
    x = x + g2_ref[...] * _mm(act, wf_ref[0])
    if final:
        x = x * lax.rsqrt(jnp.mean(x * x, axis=-1, keepdims=True) + EPS) * fw_ref[...]
    o_ref[0] = x


def _post(layer, batch, xs, ro, go, do, mods, w_out, norm2_w, w_ffn_in, w_ffn_out, final_w, final):
    t_all = xs.shape[1]
    tok = lambda n: pl.BlockSpec((1, TM, n), lambda b, t: (b, t, 0))
    wspec = lambda r, n: pl.BlockSpec((1, r, n), lambda b, t: (layer, 0, 0), pipeline_mode=pl.Buffered(1))
    return pl.pallas_call(
        functools.partial(_post_kernel, final=final),
        grid=(batch, t_all // TM),
        in_specs=[
            tok(D), tok(256), tok(256), tok(512),
            _mod_spec(layer, 2, batch), _mod_spec(layer, 3, batch),
            _mod_spec(layer, 4, batch), _mod_spec(layer, 5, batch),
            wspec(D, D),
            pl.BlockSpec((None, 1, D), lambda b, t: (layer, 0, 0)),
            wspec(D, 2 * D_FF), wspec(D_FF, D),
            pl.BlockSpec((1, D), lambda b, t: (0, 0)),
        ],
        out_specs=tok(D),
        out_shape=jax.ShapeDtypeStruct(xs.shape, F32),
        compiler_params=pltpu.CompilerParams(
            dimension_semantics=("arbitrary", "arbitrary"), vmem_limit_bytes=VMEM_LIMIT),
        name="post",
    )(xs, ro, go, do, mods, mods, mods, mods, w_out, norm2_w, w_ffn_in, w_ffn_out, final_w)


def _rope_tables(n_lat):
    def pack(cos16_by_lane, sin16_by_lane):
        lane = np.arange(LANE)
        first = (lane % 32) < 16
        cos = cos16_by_lane
        sa = jnp.where(first[None, :], -sin16_by_lane, 0.0)
        sb = jnp.where(first[None, :], 0.0, sin16_by_lane)
        ident = (jnp.ones((CTX, LANE), F32), jnp.zeros((CTX, LANE), F32), jnp.zeros((CTX, LANE), F32))
        return tuple(jnp.concatenate([i, t.astype(F32)], axis=0) for i, t in zip(ident, (cos, sa, sb)))

    lane = np.arange(LANE)
    pos = jnp.arange(n_lat, dtype=F32)
    ret_freq = 1.0 / (ROPE_BASE ** jnp.linspace(0.0, 1.0, RET_DK // 2, dtype=F32))
    ang = pos[:, None] * ret_freq[None, :]
    ang = ang[:, lane % 16]
    ret = pack(jnp.cos(ang), jnp.sin(ang))
    ax_freq = 1.0 / (ROPE_BASE ** (jnp.arange(DIFF_HD // 4, dtype=F32) / (DIFF_HD // 4)))
    idx = jnp.arange(n_lat)
    row_ang = (idx // GRID_W).astype(F32)[:, None] * ax_freq[None, :]
    col_ang = (idx % GRID_W).astype(F32)[:, None] * ax_freq[None, :]
    is_row = ((lane % 64) < 32)[None, :]
    ang = jnp.where(is_row, row_ang[:, lane % 16], col_ang[:, lane % 16])
    diff = pack(jnp.cos(ang), jnp.sin(ang))
    return ret + diff


def kernel(x, c, ctx, c_ctx, w_ada, b_ada, norm1_w, w_in, ret_decay_logit, gla_w_gate, gla_b_gate,
           gla_norm_w, diff_lambda, diff_norm_w, w_out, norm2_w, w_ffn_in, w_ffn_out, final_norm_w):
    batch, n_lat, _ = x.shape
    assert ctx.shape[1] == CTX and n_lat % TM == 0 and batch < 16

    w_in_r = jnp.concatenate(
        [w_in[..., :1536], w_in[..., 1568:], w_in[..., 1536:1568],
         jnp.zeros((DEPTH, D, PROJ_PAD - 3104), w_in.dtype)], axis=-1).astype(BF16)
    w_out_b = w_out.astype(BF16)
    w_ffn_in_b = w_ffn_in.astype(BF16)
    w_ffn_out_b = w_ffn_out.astype(BF16)
    zpad = jnp.zeros((DEPTH, GLA_RANK, NH * GLA_DK), F32)
    zrest = jnp.zeros((DEPTH, LANE - 2 * GLA_RANK, NH * GLA_DK), F32)
    wgf = jnp.concatenate([gla_w_gate[:, 0], zpad, zrest], axis=1).astype(BF16)
    wgb = jnp.concatenate([zpad, gla_w_gate[:, 1], zrest], axis=1).astype(BF16)
    gla_nw = jnp.tile(gla_norm_w, (1, NH)).reshape(DEPTH, 1, NH * GLA_DV)
    norm1 = norm1_w.reshape(DEPTH, 1, D)
    norm2 = norm2_w.reshape(DEPTH, 1, D)
    diff_nw = diff_norm_w.reshape(DEPTH, 1, DIFF_DV)
    final_w = final_norm_w.reshape(1, D)
    tabs = _rope_tables(n_lat)

    cond = jnp.concatenate([c, c_ctx[None, :], jnp.zeros((16 - batch - 1, D), F32)], axis=0)
    mods = _ada(cond, w_ada, b_ada).reshape(DEPTH * 16 * 6, 1, D)

    xs = jnp.concatenate([ctx, x], axis=1)
    for layer in range(DEPTH):
        rq, rk, rv, rg, gq, gk, gv, gr, dq, dk, dv, lr = _proj(layer, batch, xs, mods, norm1, w_in_r, tabs)
        ro = _ret(layer, rq, rk, rv, rg, ret_decay_logit)
        go = _gla(layer, gq, gk, gv, lr, gr, wgf, wgb, gla_b_gate, gla_nw)
        do = _attn(layer, dq, dk, dv, diff_lambda, diff_nw)
        xs = _post(layer, batch, xs, ro, go, do, mods, w_out_b, norm2, w_ffn_in_b, w_ffn_out_b,
                   final_w, layer == DEPTH - 1)
    return xs[:, CTX:]
```

```python
import functools
import math

import jax
import jax.numpy as jnp
import numpy as np
from jax import lax
from jax.experimental import pallas as pl
from jax.experimental.pallas import tpu as pltpu

F32 = jnp.float32
BF16 = jnp.bfloat16

D = 1024
DEPTH = 4
CTX = 256
GRID_W = 64
EPS = 1e-6
GN_EPS = 1e-5
ROPE_BASE = 10000.0

NH = 4
RET_DK, RET_DV = 32, 64
GLA_DK, GLA_DV = 32, 64
GLA_RANK = 16
GLA_TAU = 16.0
DIFF_HD, DIFF_DV = 64, 128
D_FF = 2816

O_RQ, O_RK, O_RV, O_RG = 0, 128, 256, 512
O_GQ, O_GK, O_GV, O_GR = 768, 896, 1024, 1280
O_DQ, O_DK, O_DV, O_LR = 1536, 2048, 2560, 3072
PROJ_PAD = 3200

LANE = 128
TM = 256
TQ = 256
RET_C = 128
GLA_C = 128
CHUNK_UNROLL = 3
VMEM_LIMIT = 56 * 1024 * 1024


def _mm(a, b):
    return jnp.dot(a, b, preferred_element_type=F32)


def _mm_nt(a, b):
    return lax.dot_general(a, b, (((1,), (1,)), ((), ())), preferred_element_type=F32)


def _mm_tn(a, b):
    return lax.dot_general(a, b, (((0,), (0,)), ((), ())), preferred_element_type=F32)


def _split(x):
    hi = x.astype(BF16)
    lo = (x - hi.astype(F32)).astype(BF16)
    return hi, lo


def _mm_hilo(x, m_bf16):
    hi, lo = _split(x)
    return _mm(hi, m_bf16) + _mm(lo, m_bf16)


def _hilo_mm(m_bf16, x):
    hi, lo = _split(x)
    return _mm(m_bf16, hi) + _mm(m_bf16, lo)


def _log_sigmoid(z):
    return jnp.minimum(z, 0.0) - jnp.log(1.0 + jnp.exp(-jnp.abs(z)))


def _silu(z):
    return z / (1.0 + jnp.exp(-z))


def _iota(shape, dim):
    return lax.broadcasted_iota(jnp.int32, shape, dim)


def _group_avg_matrix(n, group_shift):
    r = _iota((n, n), 0) >> group_shift
    c = _iota((n, n), 1) >> group_shift
    return jnp.where(r == c, 1.0 / (1 << group_shift), 0.0).astype(BF16)


def _ada_kernel(cond_ref, w_ref, b_ref, o_ref):
    cond = _silu(cond_ref[...])
    w = w_ref[0]
    c_hi, c_lo = _split(cond)
    w_hi, w_lo = _split(w)
    o_ref[0] = _mm(c_hi, w_hi) + _mm(c_hi, w_lo) + _mm(c_lo, w_hi) + b_ref[0]


def _ada(cond, w_ada, b_ada):
    bn = 512
    n = w_ada.shape[-1]
    return pl.pallas_call(
        _ada_kernel,
        grid=(DEPTH, n // bn),
        in_specs=[
            pl.BlockSpec((16, D), lambda l, j: (0, 0)),
            pl.BlockSpec((1, D, bn), lambda l, j: (l, 0, j)),
            pl.BlockSpec((1, 1, bn), lambda l, j: (l, 0, j)),
        ],
        out_specs=pl.BlockSpec((1, 16, bn), lambda l, j: (l, 0, j)),
        out_shape=jax.ShapeDtypeStruct((DEPTH, 16, n), F32),
        compiler_params=pltpu.CompilerParams(dimension_semantics=("arbitrary", "arbitrary")),
        name="ada",
    )(cond, w_ada, b_ada.reshape(DEPTH, 1, n))


def _mod_spec(layer, j, batch):
    def index(b, t):
        row = jnp.where(t < CTX // TM, batch, b)
        return ((layer * 16 + row) * 6 + j, 0, 0)
    return pl.BlockSpec((None, 1, D), index)


def _rope(x, cos, sa, sb):
    outs = []
    for j in range(x.shape[1] // LANE):
        blk = x[:, j * LANE:(j + 1) * LANE]
        fwd = pltpu.roll(blk, LANE - 16, 1)
        bwd = pltpu.roll(blk, 16, 1)
        outs.append(blk * cos + fwd * sa + bwd * sb)
    return outs[0] if len(outs) == 1 else jnp.concatenate(outs, axis=1)


def _proj_kernel(x_ref, shift_ref, scale_ref, nw_ref, w_ref,
                 rcos, rsa, rsb, dcos, dsa, dsb,
                 rq_o, rk_o, rv_o, rg_o, gq_o, gk_o, gv_o, gr_o, dq_o, dk_o, dv_o, lr_o):
    x = x_ref[0]
    y = x * lax.rsqrt(jnp.mean(x * x, axis=-1, keepdims=True) + EPS) * nw_ref[...]
    h = y * (1.0 + scale_ref[...]) + shift_ref[...]
    p = _mm(h.astype(BF16), w_ref[0])

    rc, ra, rb = rcos[...], rsa[...], rsb[...]
    rq_o[0] = _rope(p[:, O_RQ:O_RQ + 128], rc, ra, rb).astype(BF16)
    rk_o[0] = (_rope(p[:, O_RK:O_RK + 128], rc, ra, rb) * (RET_DK ** -0.5)).astype(BF16)
    rv = p[:, O_RV:O_RV + 256]
    rv_o[0] = (rv - _mm_hilo(rv, _group_avg_matrix(NH * RET_DV, 6))).astype(BF16)
    rg_o[0] = p[:, O_RG:O_RG + 256]
    gq_o[0] = (p[:, O_GQ:O_GQ + 128] * (GLA_DK ** -0.5)).astype(BF16)
    gk_o[0] = p[:, O_GK:O_GK + 128].astype(BF16)
    gv_o[0] = p[:, O_GV:O_GV + 256].astype(BF16)
    gr_o[0] = p[:, O_GR:O_GR + 256]
    dc, da, db = dcos[...], dsa[...], dsb[...]
    dq_o[0] = (_rope(p[:, O_DQ:O_DQ + 512], dc, da, db) * (DIFF_HD ** -0.5 * math.log2(math.e))).astype(BF16)
    dk_o[0] = _rope(p[:, O_DK:O_DK + 512], dc, da, db).astype(BF16)
    dv_o[0] = p[:, O_DV:O_DV + 512].astype(BF16)
    lr_o[0] = p[:, O_LR:O_LR + 128].astype(BF16)


def _proj(layer, batch, xs, mods, norm1_w, w_in_r, tabs):
    t_all = xs.shape[1]
    tok = lambda n: pl.BlockSpec((1, TM, n), lambda b, t: (b, t, 0))
    tab = pl.BlockSpec((TM, LANE), lambda b, t: (t, 0))
    widths = [128, 128, 256, 256, 128, 128, 256, 256, 512, 512, 512, 128]
    dtypes = [BF16, BF16, BF16, F32, BF16, BF16, BF16, F32, BF16, BF16, BF16, BF16]
    return pl.pallas_call(
        _proj_kernel,
        grid=(batch, t_all // TM),
        in_specs=[
            tok(D),
            _mod_spec(layer, 0, batch), _mod_spec(layer, 1, batch),
            pl.BlockSpec((None, 1, D), lambda b, t: (layer, 0, 0)),
            pl.BlockSpec((1, D, PROJ_PAD), lambda b, t: (layer, 0, 0), pipeline_mode=pl.Buffered(1)),
            tab, tab, tab, tab, tab, tab,
        ],
        out_specs=[tok(n) for n in widths],
        out_shape=[jax.ShapeDtypeStruct((batch, t_all, n), dt) for n, dt in zip(widths, dtypes)],
        compiler_params=pltpu.CompilerParams(
            dimension_semantics=("arbitrary", "arbitrary"), vmem_limit_bytes=VMEM_LIMIT),
        name="proj",
    )(xs, mods, mods, norm1_w, w_in_r, *tabs)


def _ret_kernel(logit_ref, q_ref, k_ref, v_ref, g_ref, o_ref, ut_f, ut_b, s_f, s_b):
    n_chunks = q_ref.shape[1] // RET_C
    n_ctx = CTX // RET_C
    c = RET_C

    lane_head = _iota((c, LANE), 1) >> 5
    i_f = _iota((c, LANE), 0).astype(F32)

    def lanes_of(d):
        z = jnp.zeros((c, LANE), F32)
        for h in range(NH):
            z = jnp.where(lane_head == h, logit_ref[d, h], z)
        return _log_sigmoid(z)

    lg_f, lg_b = lanes_of(0), lanes_of(1)
    kdec_f = jnp.exp((c - 1.0 - i_f) * lg_f)
    kdec_b = jnp.exp(i_f * lg_b)
    qdec_f = jnp.exp((i_f + 1.0) * lg_f)
    qdec_b = jnp.exp((c - i_f) * lg_b)
    cd_f = jnp.exp(c * lg_f[0:1])
    cd_b = jnp.exp(c * lg_b[0:1])

    bd = (_iota((NH * RET_DV, LANE), 0) >> 6) == (_iota((NH * RET_DV, LANE), 1) >> 5)

    def p0(n, carry):
        rows = pl.ds(pl.multiple_of(n * c, c), c)
        k = k_ref[0, rows, :].astype(F32)
        v = v_ref[0, rows, :]
        ut_f[n] = jnp.where(bd, _mm_tn(v, (k * kdec_f).astype(BF16)), 0.0)
        ut_b[n] = jnp.where(bd, _mm_tn(v, (k * kdec_b).astype(BF16)), 0.0)
        return carry

    lax.fori_loop(0, n_chunks, p0, 0, unroll=CHUNK_UNROLL)

    s = jnp.zeros((NH * RET_DV, LANE), F32)
    for n in range(n_chunks):
        s_f[n] = s.astype(BF16)
        s = s * cd_f + ut_f[n]
    s = jnp.zeros((NH * RET_DV, LANE), F32)
    for n in list(range(n_ctx - 1, -1, -1)) + list(range(n_chunks - 1, n_ctx - 1, -1)):
        s_b[n] = s.astype(BF16)
        s = s * cd_b + ut_b[n]

    rel = (_iota((c, c), 0) - _iota((c, c), 1)).astype(F32)
    eye = jnp.where(rel == 0.0, 1.0, 0.0)
    dmats = []
    for h in range(NH):
        lf = _log_sigmoid(jnp.full((c, c), logit_ref[0, h], F32))
        lb = _log_sigmoid(jnp.full((c, c), logit_ref[1, h], F32))
        dmats.append(jnp.exp(jnp.where(rel >= 0.0, rel * lf, -rel * lb)) + eye)
    dwide = jnp.concatenate(dmats, axis=1)
    vhead = _iota((c, NH * RET_DV), 1) >> 6
    avg = _group_avg_matrix(NH * RET_DV, 6)

    def p2(n, carry):
        rows = pl.ds(pl.multiple_of(n * c, c), c)
        q = q_ref[0, rows, :]
        qf = q.astype(F32)
        k = k_ref[0, rows, :].astype(F32)
        v = v_ref[0, rows, :].astype(F32)
        kbd = jnp.concatenate([jnp.where(lane_head == h, k, 0.0).astype(BF16) for h in range(NH)], axis=0)
        vbd = jnp.concatenate([jnp.where(vhead == h, v, 0.0).astype(BF16) for h in range(NH)], axis=0)
        sc = (_mm_nt(q, kbd) * dwide).astype(BF16)
        o = (_mm(sc, vbd) + _mm_nt((qf * qdec_f).astype(BF16), s_f[n])
             + _mm_nt((qf * qdec_b).astype(BF16), s_b[n]))
        var = _mm_hilo(o * o, avg)
        o_ref[0, rows, :] = (_silu(g_ref[0, rows, :]) * o * lax.rsqrt(var + GN_EPS)).astype(BF16)
        return carry

    lax.fori_loop(0, n_chunks, p2, 0, unroll=CHUNK_UNROLL)


def _ret(layer, q, k, v, g, ret_decay_logit):
    batch, t_all, _ = q.shape
    n_chunks = t_all // RET_C
    tok = lambda n: pl.BlockSpec((1, t_all, n), lambda b, lg: (b, 0, 0))
    return pl.pallas_call(
        _ret_kernel,
        grid_spec=pltpu.PrefetchScalarGridSpec(
            num_scalar_prefetch=1,
            grid=(batch,),
            in_specs=[tok(128), tok(128), tok(256), tok(256)],
            out_specs=tok(256),
            scratch_shapes=[
                pltpu.VMEM((n_chunks, NH * RET_DV, LANE), F32),
                pltpu.VMEM((n_chunks, NH * RET_DV, LANE), F32),
                pltpu.VMEM((n_chunks, NH * RET_DV, LANE), BF16),
                pltpu.VMEM((n_chunks, NH * RET_DV, LANE), BF16),
            ],
        ),
        out_shape=jax.ShapeDtypeStruct((batch, t_all, 256), BF16),
        compiler_params=pltpu.CompilerParams(
            dimension_semantics=("arbitrary",), vmem_limit_bytes=VMEM_LIMIT),
        name="ret",
    )(ret_decay_logit[layer], q, k, v, g)


def _gla_kernel(q_ref, k_ref, v_ref, lr_ref, g_ref, wgf_ref, wgb_ref, bg_ref, nw_ref, o_ref,
                cf_s, cb_s, ut_f, ut_b, dec_f, dec_b, s_f, s_b):
    c = GLA_C
    n_chunks = q_ref.shape[1] // c
    n_ctx = CTX // c
    nd = NH * GLA_DV

    ltri = jnp.where(_iota((c, c), 0) >= _iota((c, c), 1), 1.0, 0.0)
    utri = jnp.where(_iota((c, c), 0) <= _iota((c, c), 1), 1.0, 0.0)
    ltri_b, utri_b = ltri.astype(BF16), utri.astype(BF16)
    bd = (_iota((nd, LANE), 0) >> 6) == (_iota((nd, LANE), 1) >> 5)

    lr = lr_ref[0]
    cf_s[...] = _log_sigmoid(_mm(lr, wgf_ref[...]) + bg_ref[0:1, :]) * (1.0 / GLA_TAU)
    cb_s[...] = _log_sigmoid(_mm(lr, wgb_ref[...]) + bg_ref[1:2, :]) * (1.0 / GLA_TAU)

    def p0(n, carry):
        rows = pl.ds(pl.multiple_of(n * c, c), c)
        cf = _hilo_mm(ltri_b, cf_s[rows, :])
        cb = _hilo_mm(utri_b, cb_s[rows, :])
        cf_s[rows, :] = cf
        cb_s[rows, :] = cb
        cf_end, cb_end = cf[c - 1:c, :], cb[0:1, :]
        k = k_ref[0, rows, :].astype(F32)
        v = v_ref[0, rows, :]
        ut_f[n] = jnp.where(bd, _mm_tn(v, (k * jnp.exp(cf_end - cf)).astype(BF16)), 0.0)
        ut_b[n] = jnp.where(bd, _mm_tn(v, (k * jnp.exp(cb_end - cb)).astype(BF16)), 0.0)
        dec_f[n] = jnp.broadcast_to(jnp.exp(cf_end), (8, LANE))
        dec_b[n] = jnp.broadcast_to(jnp.exp(cb_end), (8, LANE))
        return carry

    lax.fori_loop(0, n_chunks, p0, 0, unroll=CHUNK_UNROLL)

    def scan_f(n, s):
        s_f[n] = s.astype(BF16)
        return s * dec_f[n][0:1, :] + ut_f[n]

    def scan_b(i, s, hi):
        n = hi - 1 - i
        s_b[n] = s.astype(BF16)
        return s * dec_b[n][0:1, :] + ut_b[n]

    zero = jnp.zeros((nd, LANE), F32)
    lax.fori_loop(0, n_chunks, scan_f, zero)
    s_ctx = lax.fori_loop(0, n_ctx, functools.partial(scan_b, hi=n_ctx), zero)
    lax.fori_loop(0, n_chunks - n_ctx, functools.partial(scan_b, hi=n_chunks), s_ctx)

    lane_head = _iota((c, LANE), 1) >> 5
    vhead = _iota((c, nd), 1) >> 6
    avg = _group_avg_matrix(nd, 6)
    nw = nw_ref[...]
    ltri_w = jnp.concatenate([ltri] * NH, axis=1)
    utri_w = jnp.concatenate([utri] * NH, axis=1)

    def stack_heads(x, head_of_lane):
        return jnp.concatenate([jnp.where(head_of_lane == h, x, 0.0).astype(BF16) for h in range(NH)], axis=0)

    def p2(n, carry):
        rows = pl.ds(pl.multiple_of(n * c, c), c)
        q = q_ref[0, rows, :].astype(F32)
        k = k_ref[0, rows, :].astype(F32)
        v = v_ref[0, rows, :].astype(F32)
        cf, cb = cf_s[rows, :], cb_s[rows, :]
        rf, rb = cf[c // 2 - 1:c // 2, :], cb[c // 2:c // 2 + 1, :]
        qf, kf = (q * jnp.exp(cf - rf)).astype(BF16), stack_heads(k * jnp.exp(rf - cf), lane_head)
        qb, kb = (q * jnp.exp(cb - rb)).astype(BF16), stack_heads(k * jnp.exp(rb - cb), lane_head)
        a = (_mm_nt(qf, kf) * ltri_w + _mm_nt(qb, kb) * utri_w).astype(BF16)
        o = (_mm(a, stack_heads(v, vhead))
             + _mm_nt((q * jnp.exp(cf)).astype(BF16), s_f[n])
             + _mm_nt((q * jnp.exp(cb)).astype(BF16), s_b[n]))
        ms = _mm_hilo(o * o, avg)
        y = o * lax.rsqrt(ms + EPS) * nw
        o_ref[0, rows, :] = (_silu(g_ref[0, rows, :]) * y).astype(BF16)
        return carry

    lax.fori_loop(0, n_chunks, p2, 0, unroll=CHUNK_UNROLL)


def _gla(layer, q, k, v, lr, g, wgf, wgb, bg, nw):
    batch, t_all, _ = q.shape
    n_chunks = t_all // GLA_C
    nd = NH * GLA_DV
    tok = lambda n: pl.BlockSpec((1, t_all, n), lambda b: (b, 0, 0))
    par = lambda r, n: pl.BlockSpec((None, r, n), lambda b: (layer, 0, 0))
    return pl.pallas_call(
        _gla_kernel,
        grid=(batch,),
        in_specs=[tok(128), tok(128), tok(256), tok(128), tok(256),
                  par(128, 128), par(128, 128), par(2, 128), par(1, 256)],
        out_specs=tok(256),
        out_shape=jax.ShapeDtypeStruct((batch, t_all, 256), BF16),
        scratch_shapes=[
            pltpu.VMEM((t_all, LANE), F32), pltpu.VMEM((t_all, LANE), F32),
            pltpu.VMEM((n_chunks, nd, LANE), F32), pltpu.VMEM((n_chunks, nd, LANE), F32),
            pltpu.VMEM((n_chunks, 8, LANE), F32), pltpu.VMEM((n_chunks, 8, LANE), F32),
            pltpu.VMEM((n_chunks, nd, LANE), BF16), pltpu.VMEM((n_chunks, nd, LANE), BF16),
        ],
        compiler_params=pltpu.CompilerParams(
            dimension_semantics=("arbitrary",), vmem_limit_bytes=VMEM_LIMIT),
        name="gla",
    )(q, k, v, lr, g, wgf, wgb, bg, nw)


def _attn_kernel(q_ref, k_ref, v_ref, lam_ref, nw_ref, o_ref, vx_ref, *, lam_init):
    t_all = k_ref.shape[1]
    vx_ref[:, 0:DIFF_DV] = v_ref[0]
    vx_ref[:, DIFF_DV:] = jnp.ones((t_all, DIFF_DV), BF16)
    lp = lam_ref[...]
    lam = (jnp.exp(jnp.sum(lp[0:1] * lp[1:2], axis=-1, keepdims=True))
           - jnp.exp(jnp.sum(lp[2:3] * lp[3:4], axis=-1, keepdims=True)) + lam_init)
    first = _iota((TQ, LANE), 1) < DIFF_HD

    def soft_pv(qh, n_keys):
        s = _mm_nt(qh, k_ref[0, 0:n_keys, :]).astype(BF16)
        e = jnp.exp2(s - jnp.max(s, axis=-1, keepdims=True))
        ol = _mm(e, vx_ref[0:n_keys, :])
        return ol[:, 0:DIFF_DV] / ol[:, DIFF_DV:]

    def attend(row0, n_keys):
        rows = pl.ds(row0, TQ)
        q = q_ref[0, rows, :]
        q1 = jnp.where(first, q, jnp.zeros_like(q))
        q2 = jnp.where(first, jnp.zeros_like(q), q)
        o = soft_pv(q1, n_keys) - lam * soft_pv(q2, n_keys)
        y = o * lax.rsqrt(jnp.mean(o * o, axis=-1, keepdims=True) + EPS) * nw_ref[...]
        o_ref[0, rows, :] = (y * (1.0 - lam_init)).astype(BF16)

    attend(0, CTX)

    def tile(i, carry):
        attend(pl.multiple_of(i * TQ, TQ), t_all)
        return carry

    lax.fori_loop(CTX // TQ, t_all // TQ, tile, 0, unroll=4)


def _attn(layer, dq, dk, dv, diff_lambda, diff_norm_w):
    batch, t_all, _ = dq.shape
    lam_init = 0.8 - 0.6 * math.exp(-0.3 * layer)
    head = pl.BlockSpec((1, t_all, LANE), lambda b, h: (b, 0, h))
    return pl.pallas_call(
        functools.partial(_attn_kernel, lam_init=lam_init),
        grid=(batch, NH),
        in_specs=[
            head, head, head,
            pl.BlockSpec((None, 4, DIFF_HD), lambda b, h: (layer, 0, 0)),
            pl.BlockSpec((None, 1, DIFF_DV), lambda b, h: (layer, 0, 0)),
        ],
        out_specs=head,
        out_shape=jax.ShapeDtypeStruct((batch, t_all, NH * DIFF_DV), BF16),
        scratch_shapes=[pltpu.VMEM((t_all, 2 * DIFF_DV), BF16)],
        compiler_params=pltpu.CompilerParams(
            dimension_semantics=("arbitrary", "arbitrary"), vmem_limit_bytes=VMEM_LIMIT),
        name="attn",
    )(dq, dk, dv, diff_lambda, diff_norm_w)


def _post_kernel(x_ref, ro_ref, go_ref, do_ref, g1_ref, sh_ref, sc_ref, g2_ref,
                 wo_ref, nw_ref, wi_ref, wf_ref, fw_ref, o_ref, *, final):
    o = jnp.concatenate([ro_ref[0], go_ref[0], do_ref[0]], axis=-1)
    x = x_ref[0] + g1_ref[...] * _mm(o, wo_ref[0])
    y = x * lax.rsqrt(jnp.mean(x * x, axis=-1, keepdims=True) + EPS) * nw_ref[...]
    h = (y * (1.0 + sc_ref[...]) + sh_ref[...]).astype(BF16)
    gu = _mm(h, wi_ref[0])
    act = (_silu(gu[:, :D_FF]) * gu[:, D_FF:]).astype(BF16)
    x = x + g2_ref[...] * _mm(act, wf_ref[0])
    if final:
        x = x * lax.rsqrt(jnp.mean(x * x, axis=-1, keepdims=True) + EPS) * fw_ref[...]
    o_ref[0] = x


def _post(layer, batch, xs, ro, go, do, mods, w_out, norm2_w, w_ffn_in, w_ffn_out, final_w, final):
    t_all = xs.shape[1]
    tok = lambda n: pl.BlockSpec((1, TM, n), lambda b, t: (b, t, 0))
    wspec = lambda r, n: pl.BlockSpec((1, r, n), lambda b, t: (layer, 0, 0), pipeline_mode=pl.Buffered(1))
    return pl.pallas_call(
        functools.partial(_post_kernel, final=final),
        grid=(batch, t_all // TM),
        in_specs=[
            tok(D), tok(256), tok(256), tok(512),
            _mod_spec(layer, 2, batch), _mod_spec(layer, 3, batch),
            _mod_spec(layer, 4, batch), _mod_spec(layer, 5, batch),
            wspec(D, D),
            pl.BlockSpec((None, 1, D), lambda b, t: (layer, 0, 0)),
            wspec(D, 2 * D_FF), wspec(D_FF, D),
            pl.BlockSpec((1, D), lambda b, t: (0, 0)),
        ],
        out_specs=tok(D),
        out_shape=jax.ShapeDtypeStruct(xs.shape, F32),
        compiler_params=pltpu.CompilerParams(
            dimension_semantics=("arbitrary", "arbitrary"), vmem_limit_bytes=VMEM_LIMIT),
        name="post",
    )(xs, ro, go, do, mods, mods, mods, mods, w_out, norm2_w, w_ffn_in, w_ffn_out, final_w)


def _rope_tables(n_lat):
    def pack(cos16_by_lane, sin16_by_lane):
        lane = np.arange(LANE)
        first = (lane % 32) < 16
        cos = cos16_by_lane
        sa = jnp.where(first[None, :], -sin16_by_lane, 0.0)
        sb = jnp.where(first[None, :], 0.0, sin16_by_lane)
        ident = (jnp.ones((CTX, LANE), F32), jnp.zeros((CTX, LANE), F32), jnp.zeros((CTX, LANE), F32))
        return tuple(jnp.concatenate([i, t.astype(F32)], axis=0) for i, t in zip(ident, (cos, sa, sb)))

    lane = np.arange(LANE)
    pos = jnp.arange(n_lat, dtype=F32)
    ret_freq = 1.0 / (ROPE_BASE ** jnp.linspace(0.0, 1.0, RET_DK // 2, dtype=F32))
    ang = pos[:, None] * ret_freq[None, :]
    ang = ang[:, lane % 16]
    ret = pack(jnp.cos(ang), jnp.sin(ang))
    ax_freq = 1.0 / (ROPE_BASE ** (jnp.arange(DIFF_HD // 4, dtype=F32) / (DIFF_HD // 4)))
    idx = jnp.arange(n_lat)
    row_ang = (idx // GRID_W).astype(F32)[:, None] * ax_freq[None, :]
    col_ang = (idx % GRID_W).astype(F32)[:, None] * ax_freq[None, :]
    is_row = ((lane % 64) < 32)[None, :]
    ang = jnp.where(is_row, row_ang[:, lane % 16], col_ang[:, lane % 16])
    diff = pack(jnp.cos(ang), jnp.sin(ang))
    return ret + diff


def kernel(x, c, ctx, c_ctx, w_ada, b_ada, norm1_w, w_in, ret_decay_logit, gla_w_gate, gla_b_gate,
           gla_norm_w, diff_lambda, diff_norm_w, w_out, norm2_w, w_ffn_in, w_ffn_out, final_norm_w):
    batch, n_lat, _ = x.shape
    assert ctx.shape[1] == CTX and n_lat % TM == 0 and batch < 16

    w_in_r = jnp.concatenate(
        [w_in[..., :1536], w_in[..., 1568:], w_in[..., 1536:1568],
         jnp.zeros((DEPTH, D, PROJ_PAD - 3104), w_in.dtype)], axis=-1).astype(BF16)
    w_out_b = w_out.astype(BF16)
    w_ffn_in_b = w_ffn_in.astype(BF16)
    w_ffn_out_b = w_ffn_out.astype(BF16)
    zpad = jnp.zeros((DEPTH, GLA_RANK, NH * GLA_DK), F32)
    zrest = jnp.zeros((DEPTH, LANE - 2 * GLA_RANK, NH * GLA_DK), F32)
    wgf = jnp.concatenate([gla_w_gate[:, 0], zpad, zrest], axis=1).astype(BF16)
    wgb = jnp.concatenate([zpad, gla_w_gate[:, 1], zrest], axis=1).astype(BF16)
    gla_nw = jnp.tile(gla_norm_w, (1, NH)).reshape(DEPTH, 1, NH * GLA_DV)
    norm1 = norm1_w.reshape(DEPTH, 1, D)
    norm2 = norm2_w.reshape(DEPTH, 1, D)
    diff_nw = diff_norm_w.reshape(DEPTH, 1, DIFF_DV)
    final_w = final_norm_w.reshape(1, D)
    tabs = _rope_tables(n_lat)

    cond = jnp.concatenate([c, c_ctx[None, :], jnp.zeros((16 - batch - 1, D), F32)], axis=0)
    mods = _ada(cond, w_ada, b_ada).reshape(DEPTH * 16 * 6, 1, D)

    xs = jnp.concatenate([ctx, x], axis=1)
    for layer in range(DEPTH):
        rq, rk, rv, rg, gq, gk, gv, gr, dq, dk, dv, lr = _proj(layer, batch, xs, mods, norm1, w_in_r, tabs)
        ro = _ret(layer, rq, rk, rv, rg, ret_decay_logit)
        go = _gla(layer, gq, gk, gv, lr, gr, wgf, wgb, gla_b_gate, gla_nw)
        do = _attn(layer, dq, dk, dv, diff_lambda, diff_nw)
        xs = _post(layer, batch, xs, ro, go, do, mods, w_out_b, norm2, w_ffn_in_b, w_ffn_out_b,
                   final_w, layer == DEPTH - 1)
    return xs[:, CTX:]
```

```python
import functools
import math

import jax
import jax.numpy as jnp
import numpy as np
from jax import lax
from jax.experimental import pallas as pl
from jax.experimental.pallas import tpu as pltpu

F32 = jnp.float32
BF16 = jnp.bfloat16

D = 1024
DEPTH = 4
CTX = 256
GRID_W = 64
EPS = 1e-6
GN_EPS = 1e-5
ROPE_BASE = 10000.0

NH = 4
RET_DK, RET_DV = 32, 64
GLA_DK, GLA_DV = 32, 64
GLA_RANK = 16
GLA_TAU = 16.0
DIFF_HD, DIFF_DV = 64, 128
D_FF = 2816

O_RQ, O_RK, O_RV, O_RG = 0, 128, 256, 512
O_GQ, O_GK, O_GV, O_GR = 768, 896, 1024, 1280
O_DQ, O_DK, O_DV, O_LR = 1536, 2048, 2560, 3072
PROJ_PAD = 3200

LANE = 128
TM = 256
TQ = 256
RET_C = 128
GLA_C = 128
CHUNK_UNROLL = 6
CHUNK_GROUP = 6
ATTN_GROUP = 8
VMEM_LIMIT = 56 * 1024 * 1024


def _mm(a, b):
    return jnp.dot(a, b, preferred_element_type=F32)


def _mm_nt(a, b):
    return lax.dot_general(a, b, (((1,), (1,)), ((), ())), preferred_element_type=F32)


def _mm_tn(a, b):
    return lax.dot_general(a, b, (((0,), (0,)), ((), ())), preferred_element_type=F32)


def _split(x):
    hi = x.astype(BF16)
    lo = (x - hi.astype(F32)).astype(BF16)
    return hi, lo


def _hilo_mm(m_bf16, x):
    hi, lo = _split(x)
    n = x.shape[1]
    r = _mm(m_bf16, jnp.concatenate([hi, lo], axis=1))
    return r[:, :n] + r[:, n:]


def _group_mean(x, avg_bf16):
    hi, lo = _split(x)
    m = x.shape[0]
    r = _mm(jnp.concatenate([hi, lo], axis=0), avg_bf16)
    return r[:m] + r[m:]


def _log_sigmoid(z):
    return jnp.minimum(z, 0.0) - jnp.log(1.0 + jnp.exp(-jnp.abs(z)))


def _silu(z):
    return z / (1.0 + jnp.exp(-z))


def _iota(shape, dim):
    return lax.broadcasted_iota(jnp.int32, shape, dim)


def _group_avg_matrix(n, group_shift):
    r = _iota((n, n), 0) >> group_shift
    c = _iota((n, n), 1) >> group_shift
    return jnp.where(r == c, 1.0 / (1 << group_shift), 0.0).astype(BF16)


def _ada_kernel(cond_ref, w_ref, b_ref, o_ref):
    cond = _silu(cond_ref[...])
    w = w_ref[0]
    c_hi, c_lo = _split(cond)
    w_hi, w_lo = _split(w)
    o_ref[0] = _mm(c_hi, w_hi) + _mm(c_hi, w_lo) + _mm(c_lo, w_hi) + b_ref[0]


def _ada(cond, w_ada, b_ada):
    bn = 1536
    n = w_ada.shape[-1]
    return pl.pallas_call(
        _ada_kernel,
        grid=(DEPTH, n // bn),
        in_specs=[
            pl.BlockSpec((16, D), lambda l, j: (0, 0)),
            pl.BlockSpec((1, D, bn), lambda l, j: (l, 0, j)),
            pl.BlockSpec((1, 1, bn), lambda l, j: (l, 0, j)),
        ],
        out_specs=pl.BlockSpec((1, 16, bn), lambda l, j: (l, 0, j)),
        out_shape=jax.ShapeDtypeStruct((DEPTH, 16, n), F32),
        compiler_params=pltpu.CompilerParams(dimension_semantics=("arbitrary", "arbitrary")),
        name="ada",
    )(cond, w_ada, b_ada.reshape(DEPTH, 1, n))


def _mod_spec(layer, j, batch, t0=0):
    def index(b, t):
        row = jnp.where(t + t0 < CTX // TM, batch, b)
        return ((layer * 16 + row) * 6 + j, 0, 0)
    return pl.BlockSpec((None, 1, D), index)


def _stream_specs(stream, t0=0):
    n_ctx = CTX // TM
    if isinstance(stream, tuple):
        return [pl.BlockSpec((1, TM, D), lambda b, t: (b, jnp.minimum(t + t0, n_ctx - 1), 0)),
                pl.BlockSpec((1, TM, D), lambda b, t: (b, jnp.maximum(t + t0 - n_ctx, 0), 0))], list(stream)
    return [pl.BlockSpec((1, TM, D), lambda b, t: (b, t + t0, 0))], [stream]


def _load_stream(refs, t0=0):
    if len(refs) == 2:
        return jnp.where(pl.program_id(1) + t0 < CTX // TM, refs[0][0], refs[1][0])
    return refs[0][0]


def _rope(x, cos, sa, sb):
    outs = []
    for j in range(x.shape[1] // LANE):
        blk = x[:, j * LANE:(j + 1) * LANE]
        fwd = pltpu.roll(blk, LANE - 16, 1)
        bwd = pltpu.roll(blk, 16, 1)
        outs.append(blk * cos + fwd * sa + bwd * sb)
    return outs[0] if len(outs) == 1 else jnp.concatenate(outs, axis=1)


def _proj_kernel(*refs, n_stream):
    (shift_ref, scale_ref, nw_ref, w_ref, rcos, rsa, rsb, dcos, dsa, dsb,
     rq_o, rk_o, rv_o, rg_o, gq_o, gk_o, gv_o, gr_o, dq_o, dk_o, dv_o, lr_o) = refs[n_stream:]
    x = _load_stream(refs[:n_stream])
    y = x * lax.rsqrt(jnp.mean(x * x, axis=-1, keepdims=True) + EPS) * nw_ref[...]
    h = y * (1.0 + scale_ref[...]) + shift_ref[...]
    p = _mm(h.astype(BF16), w_ref[0])

    rc, ra, rb = rcos[...], rsa[...], rsb[...]
    rq_o[0] = _rope(p[:, O_RQ:O_RQ + 128], rc, ra, rb).astype(BF16)
    rk_o[0] = (_rope(p[:, O_RK:O_RK + 128], rc, ra, rb) * (RET_DK ** -0.5)).astype(BF16)
    rv = p[:, O_RV:O_RV + 256]
    rv_o[0] = (rv - _group_mean(rv, _group_avg_matrix(NH * RET_DV, 6))).astype(BF16)
    rg_o[0] = p[:, O_RG:O_RG + 256]
    gq_o[0] = (p[:, O_GQ:O_GQ + 128] * (GLA_DK ** -0.5)).astype(BF16)
    gk_o[0] = p[:, O_GK:O_GK + 128].astype(BF16)
    gv_o[0] = p[:, O_GV:O_GV + 256].astype(BF16)
    gr_o[0] = p[:, O_GR:O_GR + 256]
    dc, da, db = dcos[...], dsa[...], dsb[...]
    dq_o[0] = (_rope(p[:, O_DQ:O_DQ + 512], dc, da, db) * (DIFF_HD ** -0.5 * math.log2(math.e))).astype(BF16)
    dk_o[0] = _rope(p[:, O_DK:O_DK + 512], dc, da, db).astype(BF16)
    dv_o[0] = p[:, O_DV:O_DV + 512].astype(BF16)
    lr_o[0] = p[:, O_LR:O_LR + 128].astype(BF16)


def _proj(layer, batch, t_all, stream, mods, norm1_w, w_in_r, tabs):
    tok = lambda n: pl.BlockSpec((1, TM, n), lambda b, t: (b, t, 0))
    tab = pl.BlockSpec((TM, LANE), lambda b, t: (t, 0))
    widths = [128, 128, 256, 256, 128, 128, 256, 256, 512, 512, 512, 128]
    dtypes = [BF16, BF16, BF16, F32, BF16, BF16, BF16, F32, BF16, BF16, BF16, BF16]
    stream_specs, stream_ops = _stream_specs(stream)
    return pl.pallas_call(
        functools.partial(_proj_kernel, n_stream=len(stream_ops)),
        grid=(batch, t_all // TM),
        in_specs=stream_specs + [
            _mod_spec(layer, 0, batch), _mod_spec(layer, 1, batch),
            pl.BlockSpec((None, 1, D), lambda b, t: (layer, 0, 0)),
            pl.BlockSpec((1, D, PROJ_PAD), lambda b, t: (layer, 0, 0), pipeline_mode=pl.Buffered(1)),
            tab, tab, tab, tab, tab, tab,
        ],
        out_specs=[tok(n) for n in widths],
        out_shape=[jax.ShapeDtypeStruct((batch, t_all, n), dt) for n, dt in zip(widths, dtypes)],
        compiler_params=pltpu.CompilerParams(
            dimension_semantics=("arbitrary", "arbitrary"), vmem_limit_bytes=VMEM_LIMIT),
        name="proj",
    )(*stream_ops, mods, mods, norm1_w, w_in_r, *tabs)


def _ret_kernel(logit_ref, q_ref, k_ref, v_ref, g_ref, o_ref, ut, st):
    n_chunks = q_ref.shape[1] // RET_C
    n_ctx = CTX // RET_C
    c = RET_C

    lane_head = _iota((c, LANE), 1) >> 5
    i_f = _iota((c, LANE), 0).astype(F32)

    def lanes_of(d):
        z = jnp.zeros((c, LANE), F32)
        for h in range(NH):
            z = jnp.where(lane_head == h, logit_ref[d, h], z)
        return _log_sigmoid(z)

    lg_f, lg_b = lanes_of(0), lanes_of(1)
    kdec_f = jnp.exp((c - 1.0 - i_f) * lg_f)
    kdec_b = jnp.exp(i_f * lg_b)
    qdec_f = jnp.exp((i_f + 1.0) * lg_f)
    qdec_b = jnp.exp((c - i_f) * lg_b)
    cd_f = jnp.exp(c * lg_f[0:1])
    cd_b = jnp.exp(c * lg_b[0:1])
    kdec = jnp.concatenate([kdec_f, kdec_b], axis=1)
    qdec = jnp.concatenate([qdec_f, qdec_b], axis=1)

    nd = NH * RET_DV
    bd = (_iota((nd, 2 * LANE), 0) >> 6) == ((_iota((nd, 2 * LANE), 1) & (LANE - 1)) >> 5)

    def p0(n, carry):
        rows = pl.ds(pl.multiple_of(n * c, c), c)
        k = k_ref[0, rows, :].astype(F32)
        k2 = (jnp.concatenate([k, k], axis=1) * kdec).astype(BF16)
        ut[n] = jnp.where(bd, _mm_tn(v_ref[0, rows, :], k2), 0.0)
        return carry

    lax.fori_loop(0, n_chunks, p0, 0, unroll=CHUNK_UNROLL)

    s = jnp.zeros((nd, LANE), F32)
    for n in range(n_chunks):
        st[n, :, 0:LANE] = s.astype(BF16)
        s = s * cd_f + ut[n, :, 0:LANE]
    s = jnp.zeros((nd, LANE), F32)
    for n in list(range(n_ctx - 1, -1, -1)) + list(range(n_chunks - 1, n_ctx - 1, -1)):
        st[n, :, LANE:] = s.astype(BF16)
        s = s * cd_b + ut[n, :, LANE:]

    rel = (_iota((c, c), 0) - _iota((c, c), 1)).astype(F32)
    eye = jnp.where(rel == 0.0, 1.0, 0.0)
    dmats = []
    for h in range(NH):
        lf = _log_sigmoid(jnp.full((c, c), logit_ref[0, h], F32))
        lb = _log_sigmoid(jnp.full((c, c), logit_ref[1, h], F32))
        dmats.append(jnp.exp(jnp.where(rel >= 0.0, rel * lf, -rel * lb)) + eye)
    dwide = jnp.concatenate(dmats, axis=1)
    vhead = _iota((c, NH * RET_DV), 1) >> 6
    avg = _group_avg_matrix(NH * RET_DV, 6)

    def p2(i, carry):
        ns = [i * CHUNK_GROUP + j for j in range(CHUNK_GROUP)]
        rows = [pl.ds(pl.multiple_of(n * c, c), c) for n in ns]
        sc, inter = [], []
        for n, r in zip(ns, rows):
            q = q_ref[0, r, :]
            k = k_ref[0, r, :].astype(F32)
            kbd = jnp.concatenate([jnp.where(lane_head == h, k, 0.0).astype(BF16) for h in range(NH)], axis=0)
            sc.append(_mm_nt(q, kbd))
            qf = q.astype(F32)
            inter.append(_mm_nt((jnp.concatenate([qf, qf], axis=1) * qdec).astype(BF16), st[n]))
        o = []
        for r, s_c, o_c in zip(rows, sc, inter):
            v = v_ref[0, r, :].astype(F32)
            vbd = jnp.concatenate([jnp.where(vhead == h, v, 0.0).astype(BF16) for h in range(NH)], axis=0)
            o.append(_mm((s_c * dwide).astype(BF16), vbd) + o_c)
        var = [_group_mean(o_c * o_c, avg) for o_c in o]
        for r, o_c, v_c in zip(rows, o, var):
            o_ref[0, r, :] = (_silu(g_ref[0, r, :]) * o_c * lax.rsqrt(v_c + GN_EPS)).astype(BF16)
        return carry

    lax.fori_loop(0, n_chunks // CHUNK_GROUP, p2, 0)


def _ret(layer, q, k, v, g, ret_decay_logit):
    batch, t_all, _ = q.shape
    n_chunks = t_all // RET_C
    tok = lambda n: pl.BlockSpec((1, t_all, n), lambda b, lg: (b, 0, 0))
    return pl.pallas_call(
        _ret_kernel,
        grid_spec=pltpu.PrefetchScalarGridSpec(
            num_scalar_prefetch=1,
            grid=(batch,),
            in_specs=[tok(128), tok(128), tok(256), tok(256)],
            out_specs=tok(256),
            scratch_shapes=[
                pltpu.VMEM((n_chunks, NH * RET_DV, 2 * LANE), F32),
                pltpu.VMEM((n_chunks, NH * RET_DV, 2 * LANE), BF16),
            ],
        ),
        out_shape=jax.ShapeDtypeStruct((batch, t_all, 256), BF16),
        compiler_params=pltpu.CompilerParams(
            dimension_semantics=("arbitrary",), vmem_limit_bytes=VMEM_LIMIT),
        name="ret",
    )(ret_decay_logit[layer], q, k, v, g)


def _gla_kernel(q_ref, k_ref, v_ref, lr_ref, g_ref, wgf_ref, wgb_ref, bg_ref, nw_ref, o_ref,
                cf_s, cb_s, ut, dec, st):
    c = GLA_C
    n_chunks = q_ref.shape[1] // c
    n_ctx = CTX // c
    nd = NH * GLA_DV

    ltri = jnp.where(_iota((c, c), 0) >= _iota((c, c), 1), 1.0, 0.0)
    utri = jnp.where(_iota((c, c), 0) <= _iota((c, c), 1), 1.0, 0.0)
    ltri_b, utri_b = ltri.astype(BF16), utri.astype(BF16)
    bd = (_iota((nd, 2 * LANE), 0) >> 6) == ((_iota((nd, 2 * LANE), 1) & (LANE - 1)) >> 5)

    lr = lr_ref[0]
    cf_s[...] = _log_sigmoid(_mm(lr, wgf_ref[...]) + bg_ref[0:1, :]) * (1.0 / GLA_TAU)
    cb_s[...] = _log_sigmoid(_mm(lr, wgb_ref[...]) + bg_ref[1:2, :]) * (1.0 / GLA_TAU)

    def p0(i, carry):
        ns = [i * CHUNK_GROUP + j for j in range(CHUNK_GROUP)]
        rows = [pl.ds(pl.multiple_of(n * c, c), c) for n in ns]
        cf = [_hilo_mm(ltri_b, cf_s[r, :]) for r in rows]
        cb = [_hilo_mm(utri_b, cb_s[r, :]) for r in rows]
        for n, r, cf_c, cb_c in zip(ns, rows, cf, cb):
            cf_s[r, :] = cf_c
            cb_s[r, :] = cb_c
            cf_end, cb_end = cf_c[c - 1:c, :], cb_c[0:1, :]
            k = k_ref[0, r, :].astype(F32)
            k2 = jnp.concatenate([k * jnp.exp(cf_end - cf_c), k * jnp.exp(cb_end - cb_c)], axis=1).astype(BF16)
            ut[n] = jnp.where(bd, _mm_tn(v_ref[0, r, :], k2), 0.0)
            dec[n] = jnp.broadcast_to(jnp.exp(jnp.concatenate([cf_end, cb_end], axis=1)), (8, 2 * LANE))
        return carry

    lax.fori_loop(0, n_chunks // CHUNK_GROUP, p0, 0)

    def scan_f(n, s):
        st[n, :, 0:LANE] = s.astype(BF16)
        return s * dec[n, 0:1, 0:LANE] + ut[n, :, 0:LANE]

    def scan_b(i, s, hi):
        n = hi - 1 - i
        st[n, :, LANE:] = s.astype(BF16)
        return s * dec[n, 0:1, LANE:] + ut[n, :, LANE:]

    zero = jnp.zeros((nd, LANE), F32)
    lax.fori_loop(0, n_chunks, scan_f, zero)
    s_ctx = lax.fori_loop(0, n_ctx, functools.partial(scan_b, hi=n_ctx), zero)
    lax.fori_loop(0, n_chunks - n_ctx, functools.partial(scan_b, hi=n_chunks), s_ctx)

    lane_head = _iota((c, LANE), 1) >> 5
    vhead = _iota((c, nd), 1) >> 6
    avg = _group_avg_matrix(nd, 6)
    nw = nw_ref[...]
    ltri_w = jnp.concatenate([ltri] * NH, axis=1)
    utri_w = jnp.concatenate([utri] * NH, axis=1)

    def stack_heads(x, head_of_lane):
        return jnp.concatenate([jnp.where(head_of_lane == h, x, 0.0).astype(BF16) for h in range(NH)], axis=0)

    def p2(i, carry):
        ns = [i * CHUNK_GROUP + j for j in range(CHUNK_GROUP)]
        rows = [pl.ds(pl.multiple_of(n * c, c), c) for n in ns]
        s_fb, inter = [], []
        for n, r in zip(ns, rows):
            q = q_ref[0, r, :].astype(F32)
            k = k_ref[0, r, :].astype(F32)
            cf, cb = cf_s[r, :], cb_s[r, :]
            rf, rb = cf[c // 2 - 1:c // 2, :], cb[c // 2:c // 2 + 1, :]
            qf, qb = (q * jnp.exp(cf - rf)).astype(BF16), (q * jnp.exp(cb - rb)).astype(BF16)
            zq = jnp.zeros_like(qf)
            q_fb = jnp.concatenate([jnp.concatenate([qf, zq], axis=1), jnp.concatenate([zq, qb], axis=1)], axis=0)
            k_fb = jnp.concatenate([stack_heads(k * jnp.exp(rf - cf), lane_head),
                                    stack_heads(k * jnp.exp(rb - cb), lane_head)], axis=1)
            s_fb.append(_mm_nt(q_fb, k_fb))
            q2 = jnp.concatenate([q * jnp.exp(cf), q * jnp.exp(cb)], axis=1).astype(BF16)
            inter.append(_mm_nt(q2, st[n]))
        o = []
        for r, s_c, o_c in zip(rows, s_fb, inter):
            a = (s_c[:c] * ltri_w + s_c[c:] * utri_w).astype(BF16)
            o.append(_mm(a, stack_heads(v_ref[0, r, :].astype(F32), vhead)) + o_c)
        ms = [_group_mean(o_c * o_c, avg) for o_c in o]
        for r, o_c, ms_c in zip(rows, o, ms):
            y = o_c * lax.rsqrt(ms_c + EPS) * nw
            o_ref[0, r, :] = (_silu(g_ref[0, r, :]) * y).astype(BF16)
        return carry

    lax.fori_loop(0, n_chunks // CHUNK_GROUP, p2, 0)


def _gla(layer, q, k, v, lr, g, wgf, wgb, bg, nw):
    batch, t_all, _ = q.shape
    n_chunks = t_all // GLA_C
    nd = NH * GLA_DV
    tok = lambda n: pl.BlockSpec((1, t_all, n), lambda b: (b, 0, 0))
    par = lambda r, n: pl.BlockSpec((None, r, n), lambda b: (layer, 0, 0))
    return pl.pallas_call(
        _gla_kernel,
        grid=(batch,),
        in_specs=[tok(128), tok(128), tok(256), tok(128), tok(256),
                  par(128, 128), par(128, 128), par(2, 128), par(1, 256)],
        out_specs=tok(256),
        out_shape=jax.ShapeDtypeStruct((batch, t_all, 256), BF16),
        scratch_shapes=[
            pltpu.VMEM((t_all, LANE), F32), pltpu.VMEM((t_all, LANE), F32),
            pltpu.VMEM((n_chunks, nd, 2 * LANE), F32),
            pltpu.VMEM((n_chunks, 8, 2 * LANE), F32),
            pltpu.VMEM((n_chunks, nd, 2 * LANE), BF16),
        ],
        compiler_params=pltpu.CompilerParams(
            dimension_semantics=("arbitrary",), vmem_limit_bytes=VMEM_LIMIT),
        name="gla",
    )(q, k, v, lr, g, wgf, wgb, bg, nw)


def _attn_kernel(q_ref, k_ref, v_ref, lam_ref, nw_ref, o_ref, vx_ref, *, lam_init):
    t_all = k_ref.shape[1]
    vx_ref[:, 0:DIFF_DV] = v_ref[0]
    vx_ref[:, DIFF_DV:] = jnp.ones((t_all, DIFF_DV), BF16)
    lp = lam_ref[...]
    lam = (jnp.exp(jnp.sum(lp[0:1] * lp[1:2], axis=-1, keepdims=True))
           - jnp.exp(jnp.sum(lp[2:3] * lp[3:4], axis=-1, keepdims=True)) + lam_init)
    first = _iota((TQ, LANE), 1) < DIFF_HD

    def attend(row0s, n_keys):
        k = k_ref[0, 0:n_keys, :]
        rows = [pl.ds(r, TQ) for r in row0s]
        s = []
        for r in rows:
            q = q_ref[0, r, :]
            for qh in (jnp.where(first, q, jnp.zeros_like(q)), jnp.where(first, jnp.zeros_like(q), q)):
                s.append(_mm_nt(qh, k).astype(BF16))
        e = [jnp.exp2(s_h - jnp.max(s_h, axis=-1, keepdims=True)) for s_h in s]
        ol = [_mm(e_h, vx_ref[0:n_keys, :]) for e_h in e]
        for j, r in enumerate(rows):
            o1, o2 = (x[:, 0:DIFF_DV] / x[:, DIFF_DV:] for x in ol[2 * j:2 * j + 2])
            o = o1 - lam * o2
            y = o * lax.rsqrt(jnp.mean(o * o, axis=-1, keepdims=True) + EPS) * nw_ref[...]
            o_ref[0, r, :] = (y * (1.0 - lam_init)).astype(BF16)

    attend([0], CTX)

    def group(i, carry):
        row0 = CTX + i * (ATTN_GROUP * TQ)
        attend([pl.multiple_of(row0 + j * TQ, TQ) for j in range(ATTN_GROUP)], t_all)
        return carry

    lax.fori_loop(0, (t_all - CTX) // (ATTN_GROUP * TQ), group, 0)


def _attn(layer, dq, dk, dv, diff_lambda, diff_norm_w):
    batch, t_all, _ = dq.shape
    lam_init = 0.8 - 0.6 * math.exp(-0.3 * layer)
    head = pl.BlockSpec((1, t_all, LANE), lambda b, h: (b, 0, h))
    return pl.pallas_call(
        functools.partial(_attn_kernel, lam_init=lam_init),
        grid=(batch, NH),
        in_specs=[
            head, head, head,
            pl.BlockSpec((None, 4, DIFF_HD), lambda b, h: (layer, 0, 0)),
            pl.BlockSpec((None, 1, DIFF_DV), lambda b, h: (layer, 0, 0)),
        ],
        out_specs=head,
        out_shape=jax.ShapeDtypeStruct((batch, t_all, NH * DIFF_DV), BF16),
        scratch_shapes=[pltpu.VMEM((t_all, 2 * DIFF_DV), BF16)],
        compiler_params=pltpu.CompilerParams(
            dimension_semantics=("arbitrary", "arbitrary"), vmem_limit_bytes=VMEM_LIMIT),
        name="attn",
    )(dq, dk, dv, diff_lambda, diff_norm_w)


def _post_kernel(*refs, n_stream, final):
    (ro_ref, go_ref, do_ref, g1_ref, sh_ref, sc_ref, g2_ref,
     wo_ref, nw_ref, wi_ref, wf_ref, fw_ref, o_ref) = refs[n_stream:]
    o = jnp.concatenate([ro_ref[0], go_ref[0], do_ref[0]], axis=-1)
    x = _load_stream(refs[:n_stream]) + g1_ref[...] * _mm(o, wo_ref[0])
    y = x * lax.rsqrt(jnp.mean(x * x, axis=-1, keepdims=True) + EPS) * nw_ref[...]
    h = (y * (1.0 + sc_ref[...]) + sh_ref[...]).astype(BF16)
    gu = _mm(h, wi_ref[0])
    act = (_silu(gu[:, :D_FF]) * gu[:, D_FF:]).astype(BF16)
    x = x + g2_ref[...] * _mm(act, wf_ref[0])
    if final:
        x = x * lax.rsqrt(jnp.mean(x * x, axis=-1, keepdims=True) + EPS) * fw_ref[...]
    o_ref[0] = x


def _post(layer, batch, t_all, stream, ro, go, do, mods, w_out, norm2_w, w_ffn_in, w_ffn_out, final_w, final):
    t0 = CTX // TM if final else 0
    assert not (final and isinstance(stream, tuple))
    tok = lambda n: pl.BlockSpec((1, TM, n), lambda b, t: (b, t + t0, 0))
    wspec = lambda r, n: pl.BlockSpec((1, r, n), lambda b, t: (layer, 0, 0), pipeline_mode=pl.Buffered(1))
    stream_specs, stream_ops = _stream_specs(stream, t0)
    return pl.pallas_call(
        functools.partial(_post_kernel, n_stream=len(stream_ops), final=final),
        grid=(batch, t_all // TM - t0),
        in_specs=stream_specs + [
            tok(256), tok(256), tok(512),
            _mod_spec(layer, 2, batch, t0), _mod_spec(layer, 3, batch, t0),
            _mod_spec(layer, 4, batch, t0), _mod_spec(layer, 5, batch, t0),
            wspec(D, D),
            pl.BlockSpec((None, 1, D), lambda b, t: (layer, 0, 0)),
            wspec(D, 2 * D_FF), wspec(D_FF, D),
            pl.BlockSpec((1, D), lambda b, t: (0, 0)),
        ],
        out_specs=pl.BlockSpec((1, TM, D), lambda b, t: (b, t, 0)),
        out_shape=jax.ShapeDtypeStruct((batch, t_all - t0 * TM, D), F32),
        compiler_params=pltpu.CompilerParams(
            dimension_semantics=("arbitrary", "arbitrary"), vmem_limit_bytes=VMEM_LIMIT),
        name="post",
    )(*stream_ops, ro, go, do, mods, mods, mods, mods, w_out, norm2_w, w_ffn_in, w_ffn_out, final_w)


def _rope_tables(n_lat):
    lane = np.arange(LANE)
    first = ((lane % 32) < 16)[None, :]

    def pack(ang):
        cos, sin = np.cos(ang), np.sin(ang)
        tabs = (cos, np.where(first, -sin, 0.0), np.where(first, 0.0, sin))
        ident = (np.ones((CTX, LANE)), np.zeros((CTX, LANE)), np.zeros((CTX, LANE)))
        return tuple(jnp.asarray(np.concatenate([i, t], axis=0), dtype=F32) for i, t in zip(ident, tabs))

    idx = np.arange(n_lat, dtype=np.float64)
    ret_freq = 1.0 / (ROPE_BASE ** np.linspace(0.0, 1.0, RET_DK // 2))
    ret = pack((idx[:, None] * ret_freq[None, :])[:, lane % 16])
    ax_freq = 1.0 / (ROPE_BASE ** (np.arange(DIFF_HD // 4) / (DIFF_HD // 4)))
    row_ang = np.floor(idx / GRID_W)[:, None] * ax_freq[None, :]
    col_ang = (idx % GRID_W)[:, None] * ax_freq[None, :]
    is_row = ((lane % 64) < 32)[None, :]
    diff = pack(np.where(is_row, row_ang[:, lane % 16], col_ang[:, lane % 16]))
    return ret + diff


def kernel(x, c, ctx, c_ctx, w_ada, b_ada, norm1_w, w_in, ret_decay_logit, gla_w_gate, gla_b_gate,
           gla_norm_w, diff_lambda, diff_norm_w, w_out, norm2_w, w_ffn_in, w_ffn_out, final_norm_w):
    batch, n_lat, _ = x.shape
    assert ctx.shape[1] == CTX and n_lat % TM == 0 and batch < 16

    w_in_r = jnp.concatenate(
        [w_in[..., :1536], w_in[..., 1568:], w_in[..., 1536:1568],
         jnp.zeros((DEPTH, D, PROJ_PAD - 3104), w_in.dtype)], axis=-1).astype(BF16)
    w_out_b = w_out.astype(BF16)
    w_ffn_in_b = w_ffn_in.astype(BF16)
    w_ffn_out_b = w_ffn_out.astype(BF16)
    zpad = jnp.zeros((DEPTH, GLA_RANK, NH * GLA_DK), F32)
    zrest = jnp.zeros((DEPTH, LANE - 2 * GLA_RANK, NH * GLA_DK), F32)
    wgf = jnp.concatenate([gla_w_gate[:, 0], zpad, zrest], axis=1).astype(BF16)
    wgb = jnp.concatenate([zpad, gla_w_gate[:, 1], zrest], axis=1).astype(BF16)
    gla_nw = jnp.tile(gla_norm_w, (1, NH)).reshape(DEPTH, 1, NH * GLA_DV)
    norm1 = norm1_w.reshape(DEPTH, 1, D)
    norm2 = norm2_w.reshape(DEPTH, 1, D)
    diff_nw = diff_norm_w.reshape(DEPTH, 1, DIFF_DV)
    final_w = final_norm_w.reshape(1, D)
    tabs = _rope_tables(n_lat)

    cond = jnp.concatenate([c, c_ctx[None, :], jnp.zeros((16 - batch - 1, D), F32)], axis=0)
    mods = _ada(cond, w_ada, b_ada).reshape(DEPTH * 16 * 6, 1, D)

    t_all = CTX + n_lat
    stream = (ctx, x)
    for layer in range(DEPTH):
        rq, rk, rv, rg, gq, gk, gv, gr, dq, dk, dv, lr = _proj(layer, batch, t_all, stream, mods, norm1, w_in_r, tabs)
        ro = _ret(layer, rq, rk, rv, rg, ret_decay_logit)
        go = _gla(layer, gq, gk, gv, lr, gr, wgf, wgb, gla_b_gate, gla_nw)
        do = _attn(layer, dq, dk, dv, diff_lambda, diff_nw)
        stream = _post(layer, batch, t_all, stream, ro, go, do, mods, w_out_b, norm2, w_ffn_in_b, w_ffn_out_b,
                       final_w, layer == DEPTH - 1)
    return stream
```

```python
import functools
import math

import jax
import jax.numpy as jnp
import numpy as np
from jax import lax
from jax.experimental import pallas as pl
from jax.experimental.pallas import tpu as pltpu

F32 = jnp.float32
BF16 = jnp.bfloat16

D = 1024
DEPTH = 4
CTX = 256
GRID_W = 64
EPS = 1e-6
GN_EPS = 1e-5
ROPE_BASE = 10000.0

NH = 4
RET_DK, RET_DV = 32, 64
GLA_DK, GLA_DV = 32, 64
GLA_RANK = 16
GLA_TAU = 16.0
DIFF_HD, DIFF_DV = 64, 128
D_FF = 2816

O_RQ, O_RK, O_RV, O_RG = 0, 128, 256, 512
O_GQ, O_GK, O_GV, O_GR = 768, 896, 1024, 1280
O_LR = 1536
N_PROJ_A = O_LR + 2 * GLA_RANK

LANE = 128
PROJ_SUBTILES = 3
POST_TILING = {"first": (256, 1, 0), "middle": (288, 2, 0), "last": (256, 2, 1)}
TQ = 256
RET_C = 128
GLA_C = 128
CHUNK_UNROLL = 6
CHUNK_GROUP = 6
ATTN_GROUP = 8
VMEM_LIMIT = 56 * 1024 * 1024


def _mm(a, b):
    return jnp.dot(a, b, preferred_element_type=F32)


def _mm_nt(a, b):
    return lax.dot_general(a, b, (((1,), (1,)), ((), ())), preferred_element_type=F32)


def _mm_tn(a, b):
    return lax.dot_general(a, b, (((0,), (0,)), ((), ())), preferred_element_type=F32)


def _split(x):
    hi = x.astype(BF16)
    lo = (x - hi.astype(F32)).astype(BF16)
    return hi, lo


def _hilo_mm(m_bf16, x):
    hi, lo = _split(x)
    n = x.shape[1]
    r = _mm(m_bf16, jnp.concatenate([hi, lo], axis=1))
    return r[:, :n] + r[:, n:]


def _group_mean(x, avg_bf16):
    hi, lo = _split(x)
    m = x.shape[0]
    r = _mm(jnp.concatenate([hi, lo], axis=0), avg_bf16)
    return r[:m] + r[m:]


def _log_sigmoid(z):
    return jnp.minimum(z, 0.0) - jnp.log(1.0 + jnp.exp(-jnp.abs(z)))


def _silu(z):
    return z / (1.0 + jnp.exp(-z))


def _iota(shape, dim):
    return lax.broadcasted_iota(jnp.int32, shape, dim)


def _group_avg_matrix(n, group_shift):
    r = _iota((n, n), 0) >> group_shift
    c = _iota((n, n), 1) >> group_shift
    return jnp.where(r == c, 1.0 / (1 << group_shift), 0.0).astype(BF16)


def _ada_kernel(cond_ref, w_ref, b_ref, o_ref):
    cond = _silu(cond_ref[...])
    w = w_ref[0]
    c_hi, c_lo = _split(cond)
    w_hi, w_lo = _split(w)
    o_ref[0] = _mm(c_hi, w_hi) + _mm(c_hi, w_lo) + _mm(c_lo, w_hi) + b_ref[0]


def _ada(cond, w_ada, b_ada):
    bn = 1536
    n = w_ada.shape[-1]
    return pl.pallas_call(
        _ada_kernel,
        grid=(DEPTH, n // bn),
        in_specs=[
            pl.BlockSpec((16, D), lambda l, j: (0, 0)),
            pl.BlockSpec((1, D, bn), lambda l, j: (l, 0, j)),
            pl.BlockSpec((1, 1, bn), lambda l, j: (l, 0, j)),
        ],
        out_specs=pl.BlockSpec((1, 16, bn), lambda l, j: (l, 0, j)),
        out_shape=jax.ShapeDtypeStruct((DEPTH, 16, n), F32),
        compiler_params=pltpu.CompilerParams(dimension_semantics=("arbitrary", "arbitrary")),
        name="ada",
    )(cond, w_ada, b_ada.reshape(DEPTH, 1, n))


def _mod_specs(layer, j, batch):
    lat = pl.BlockSpec((None, 1, D), lambda b, t: ((layer * 16 + b) * 6 + j, 0, 0))
    ctx = pl.BlockSpec((None, 1, D), lambda b, t: ((layer * 16 + batch) * 6 + j, 0, 0))
    return [lat, ctx]


class _Tiling:
    def __init__(self, sub, n_sub, s0=0):
        self.sub, self.n_sub, self.s0 = sub, n_sub, s0
        self.rows = sub * n_sub
        self.spans = [(j * sub, (j + 1) * sub) for j in range(n_sub)]

    def s(self, j, t=None):
        return (pl.program_id(1) if t is None else t) * self.n_sub + j + self.s0

    def in_specs(self, arr, width):
        specs, ops = [], []
        for j in range(self.n_sub):
            if isinstance(arr, tuple):
                assert CTX % self.sub == 0
                n_ctx = CTX // self.sub
                specs += [
                    pl.BlockSpec((1, self.sub, width), lambda b, t, j=j: (b, jnp.minimum(self.s(j, t), n_ctx - 1), 0)),
                    pl.BlockSpec((1, self.sub, width), lambda b, t, j=j: (b, jnp.maximum(self.s(j, t) - n_ctx, 0), 0))]
                ops += list(arr)
            else:
                specs.append(pl.BlockSpec((1, self.sub, width), lambda b, t, j=j: (b, self.s(j, t), 0)))
                ops.append(arr)
        return specs, ops

    def load(self, refs, j):
        if len(refs) == 2 * self.n_sub:
            return jnp.where(self.s(j) < CTX // self.sub, refs[2 * j][0], refs[2 * j + 1][0])
        return refs[j][0]

    def mod(self, lat_ref, ctx_ref, j):
        if CTX % self.sub == 0:
            return jnp.where(self.s(j) < CTX // self.sub, ctx_ref[...], lat_ref[...])
        assert self.sub > CTX and self.s0 == 0
        if j > 0:
            return lat_ref[...]
        ctx_rows = (_iota((self.sub, 1), 0) < CTX) & (pl.program_id(1) == 0)
        return jnp.where(ctx_rows, ctx_ref[...], lat_ref[...])


def _rope(x, cos, sa, sb):
    outs = []
    for j in range(x.shape[1] // LANE):
        blk = x[:, j * LANE:(j + 1) * LANE]
        fwd = pltpu.roll(blk, LANE - 16, 1)
        bwd = pltpu.roll(blk, 16, 1)
        outs.append(blk * cos + fwd * sa + bwd * sb)
    return outs[0] if len(outs) == 1 else jnp.concatenate(outs, axis=1)


def _proj_kernel(*refs, n_stream, til):
    (sh_l, sh_c, sc_l, sc_c, nw_ref, wa_ref, wd_ref, rcos, rsa, rsb, dcos, dsa, dsb,
     rq_o, rk_o, rv_o, rg_o, gq_o, gk_o, gv_o, gr_o, dq_o, dk_o, dv_o, lr_o) = refs[n_stream:]
    h = []
    for j in range(til.n_sub):
        xj = til.load(refs[:n_stream], j)
        y = xj * lax.rsqrt(jnp.mean(xj * xj, axis=-1, keepdims=True) + EPS) * nw_ref[...]
        h.append((y * (1.0 + til.mod(sc_l, sc_c, j)) + til.mod(sh_l, sh_c, j)).astype(BF16))
    p = [(_mm(hj, wa_ref[0]), _mm(hj, wd_ref[0])) for hj in h]

    avg = _group_avg_matrix(NH * RET_DV, 6)
    for (r0, r1), (pa, pd) in zip(til.spans, p):
        rc, ra, rb = rcos[r0:r1, :], rsa[r0:r1, :], rsb[r0:r1, :]
        rq_o[0, r0:r1, :] = _rope(pa[:, O_RQ:O_RQ + 128], rc, ra, rb).astype(BF16)
        rk_o[0, r0:r1, :] = (_rope(pa[:, O_RK:O_RK + 128], rc, ra, rb) * (RET_DK ** -0.5)).astype(BF16)
        rv = pa[:, O_RV:O_RV + 256]
        rv_o[0, r0:r1, :] = (rv - _group_mean(rv, avg)).astype(BF16)
        rg_o[0, r0:r1, :] = pa[:, O_RG:O_RG + 256]
        gq_o[0, r0:r1, :] = (pa[:, O_GQ:O_GQ + 128] * (GLA_DK ** -0.5)).astype(BF16)
        gk_o[0, r0:r1, :] = pa[:, O_GK:O_GK + 128].astype(BF16)
        gv_o[0, r0:r1, :] = pa[:, O_GV:O_GV + 256].astype(BF16)
        gr_o[0, r0:r1, :] = pa[:, O_GR:O_GR + 256]
        lr_o[0, r0:r1, :] = pa[:, O_LR:O_LR + 2 * GLA_RANK].astype(BF16)
        dc, da, db = dcos[r0:r1, :], dsa[r0:r1, :], dsb[r0:r1, :]
        dq_o[0, r0:r1, :] = (_rope(pd[:, 0:512], dc, da, db)
                             * (DIFF_HD ** -0.5 * math.log2(math.e))).astype(BF16)
        dk_o[0, r0:r1, :] = _rope(pd[:, 512:1024], dc, da, db).astype(BF16)
        dv_o[0, r0:r1, :] = pd[:, 1024:1536].astype(BF16)


def _proj(layer, batch, t_all, stream, mods, norm1_w, w_a, w_d, tabs):
    til = _Tiling(CTX, PROJ_SUBTILES)
    tm = til.rows
    tok = lambda n: pl.BlockSpec((1, tm, n), lambda b, t: (b, t, 0))
    tab = pl.BlockSpec((tm, LANE), lambda b, t: (t, 0))
    widths = [128, 128, 256, 256, 128, 128, 256, 256, 512, 512, 512, 2 * GLA_RANK]
    dtypes = [BF16, BF16, BF16, F32, BF16, BF16, BF16, F32, BF16, BF16, BF16, BF16]
    stream_specs, stream_ops = til.in_specs(stream, D)
    wspec = lambda n: pl.BlockSpec((1, D, n), lambda b, t: (layer, 0, 0), pipeline_mode=pl.Buffered(1))
    return pl.pallas_call(
        functools.partial(_proj_kernel, n_stream=len(stream_ops), til=til),
        grid=(batch, t_all // tm),
        in_specs=stream_specs + _mod_specs(layer, 0, batch) + _mod_specs(layer, 1, batch) + [
            pl.BlockSpec((None, 1, D), lambda b, t: (layer, 0, 0)),
            wspec(w_a.shape[-1]), wspec(w_d.shape[-1]),
            tab, tab, tab, tab, tab, tab,
        ],
        out_specs=[tok(n) for n in widths],
        out_shape=[jax.ShapeDtypeStruct((batch, t_all, n), dt) for n, dt in zip(widths, dtypes)],
        compiler_params=pltpu.CompilerParams(
            dimension_semantics=("arbitrary", "arbitrary"), vmem_limit_bytes=VMEM_LIMIT),
        name="proj",
    )(*stream_ops, mods, mods, mods, mods, norm1_w, w_a, w_d, *tabs)


def _ret_kernel(logit_ref, q_ref, k_ref, v_ref, g_ref, o_ref, ut, st):
    n_chunks = q_ref.shape[1] // RET_C
    n_ctx = CTX // RET_C
    c = RET_C

    lane_head = _iota((c, LANE), 1) >> 5
    i_f = _iota((c, LANE), 0).astype(F32)

    def lanes_of(d):
        z = jnp.zeros((c, LANE), F32)
        for h in range(NH):
            z = jnp.where(lane_head == h, logit_ref[d, h], z)
        return _log_sigmoid(z)

    lg_f, lg_b = lanes_of(0), lanes_of(1)
    kdec_f = jnp.exp((c - 1.0 - i_f) * lg_f)
    kdec_b = jnp.exp(i_f * lg_b)
    qdec_f = jnp.exp((i_f + 1.0) * lg_f)
    qdec_b = jnp.exp((c - i_f) * lg_b)
    cd_f = jnp.exp(c * lg_f[0:1])
    cd_b = jnp.exp(c * lg_b[0:1])
    kdec = jnp.concatenate([kdec_f, kdec_b], axis=1)
    qdec = jnp.concatenate([qdec_f, qdec_b], axis=1)

    nd = NH * RET_DV
    bd = (_iota((nd, 2 * LANE), 0) >> 6) == ((_iota((nd, 2 * LANE), 1) & (LANE - 1)) >> 5)

    def p0(n, carry):
        rows = pl.ds(pl.multiple_of(n * c, c), c)
        k = k_ref[0, rows, :].astype(F32)
        k2 = (jnp.concatenate([k, k], axis=1) * kdec).astype(BF16)
        ut[n] = jnp.where(bd, _mm_tn(v_ref[0, rows, :], k2), 0.0)
        return carry

    lax.fori_loop(0, n_chunks, p0, 0, unroll=CHUNK_UNROLL)

    s = jnp.zeros((nd, LANE), F32)
    for n in range(n_chunks):
        st[n, :, 0:LANE] = s.astype(BF16)
        s = s * cd_f + ut[n, :, 0:LANE]
    s = jnp.zeros((nd, LANE), F32)
    for n in list(range(n_ctx - 1, -1, -1)) + list(range(n_chunks - 1, n_ctx - 1, -1)):
        st[n, :, LANE:] = s.astype(BF16)
        s = s * cd_b + ut[n, :, LANE:]

    rel = (_iota((c, c), 0) - _iota((c, c), 1)).astype(F32)
    eye = jnp.where(rel == 0.0, 1.0, 0.0)
    dmats = []
    for h in range(NH):
        lf = _log_sigmoid(jnp.full((c, c), logit_ref[0, h], F32))
        lb = _log_sigmoid(jnp.full((c, c), logit_ref[1, h], F32))
        dmats.append(jnp.exp(jnp.where(rel >= 0.0, rel * lf, -rel * lb)) + eye)
    dwide = jnp.concatenate(dmats, axis=1)
    vhead = _iota((c, NH * RET_DV), 1) >> 6
    avg = _group_avg_matrix(NH * RET_DV, 6)

    def p2(i, carry):
        ns = [i * CHUNK_GROUP + j for j in range(CHUNK_GROUP)]
        rows = [pl.ds(pl.multiple_of(n * c, c), c) for n in ns]
        sc, inter = [], []
        for n, r in zip(ns, rows):
            q = q_ref[0, r, :]
            k = k_ref[0, r, :].astype(F32)
            kbd = jnp.concatenate([jnp.where(lane_head == h, k, 0.0).astype(BF16) for h in range(NH)], axis=0)
            sc.append(_mm_nt(q, kbd))
            qf = q.astype(F32)
            inter.append(_mm_nt((jnp.concatenate([qf, qf], axis=1) * qdec).astype(BF16), st[n]))
        o = []
        for r, s_c, o_c in zip(rows, sc, inter):
            v = v_ref[0, r, :].astype(F32)
            vbd = jnp.concatenate([jnp.where(vhead == h, v, 0.0).astype(BF16) for h in range(NH)], axis=0)
            o.append(_mm((s_c * dwide).astype(BF16), vbd) + o_c)
        var = [_group_mean(o_c * o_c, avg) for o_c in o]
        for r, o_c, v_c in zip(rows, o, var):
            o_ref[0, r, :] = (_silu(g_ref[0, r, :]) * o_c * lax.rsqrt(v_c + GN_EPS)).astype(BF16)
        return carry

    lax.fori_loop(0, n_chunks // CHUNK_GROUP, p2, 0)


def _ret(layer, q, k, v, g, ret_decay_logit):
    batch, t_all, _ = q.shape
    n_chunks = t_all // RET_C
    tok = lambda n: pl.BlockSpec((1, t_all, n), lambda b, lg: (b, 0, 0))
    return pl.pallas_call(
        _ret_kernel,
        grid_spec=pltpu.PrefetchScalarGridSpec(
            num_scalar_prefetch=1,
            grid=(batch,),
            in_specs=[tok(128), tok(128), tok(256), tok(256)],
            out_specs=tok(256),
            scratch_shapes=[
                pltpu.VMEM((n_chunks, NH * RET_DV, 2 * LANE), F32),
                pltpu.VMEM((n_chunks, NH * RET_DV, 2 * LANE), BF16),
            ],
        ),
        out_shape=jax.ShapeDtypeStruct((batch, t_all, 256), BF16),
        compiler_params=pltpu.CompilerParams(
            dimension_semantics=("arbitrary",), vmem_limit_bytes=VMEM_LIMIT),
        name="ret",
    )(ret_decay_logit[layer], q, k, v, g)


def _gla_kernel(q_ref, k_ref, v_ref, lr_ref, g_ref, wgf_ref, wgb_ref, bg_ref, nw_ref, o_ref,
                cf_s, cb_s, ut, dec, st):
    c = GLA_C
    n_chunks = q_ref.shape[1] // c
    n_ctx = CTX // c
    nd = NH * GLA_DV

    ltri = jnp.where(_iota((c, c), 0) >= _iota((c, c), 1), 1.0, 0.0)
    utri = jnp.where(_iota((c, c), 0) <= _iota((c, c), 1), 1.0, 0.0)
    ltri_b, utri_b = ltri.astype(BF16), utri.astype(BF16)
    bd = (_iota((nd, 2 * LANE), 0) >> 6) == ((_iota((nd, 2 * LANE), 1) & (LANE - 1)) >> 5)

    lr = lr_ref[0]
    cf_s[...] = _log_sigmoid(_mm(lr, wgf_ref[...]) + bg_ref[0:1, :]) * (1.0 / GLA_TAU)
    cb_s[...] = _log_sigmoid(_mm(lr, wgb_ref[...]) + bg_ref[1:2, :]) * (1.0 / GLA_TAU)

    def p0(i, carry):
        ns = [i * CHUNK_GROUP + j for j in range(CHUNK_GROUP)]
        rows = [pl.ds(pl.multiple_of(n * c, c), c) for n in ns]
        cf = [_hilo_mm(ltri_b, cf_s[r, :]) for r in rows]
        cb = [_hilo_mm(utri_b, cb_s[r, :]) for r in rows]
        for n, r, cf_c, cb_c in zip(ns, rows, cf, cb):
            cf_s[r, :] = cf_c
            cb_s[r, :] = cb_c
            cf_end, cb_end = cf_c[c - 1:c, :], cb_c[0:1, :]
            k = k_ref[0, r, :].astype(F32)
            k2 = jnp.concatenate([k * jnp.exp(cf_end - cf_c), k * jnp.exp(cb_end - cb_c)], axis=1).astype(BF16)
            ut[n] = jnp.where(bd, _mm_tn(v_ref[0, r, :], k2), 0.0)
            dec[n] = jnp.broadcast_to(jnp.exp(jnp.concatenate([cf_end, cb_end], axis=1)), (8, 2 * LANE))
        return carry

    lax.fori_loop(0, n_chunks // CHUNK_GROUP, p0, 0)

    def scan_f(n, s):
        st[n, :, 0:LANE] = s.astype(BF16)
        return s * dec[n, 0:1, 0:LANE] + ut[n, :, 0:LANE]

    def scan_b(i, s, hi):
        n = hi - 1 - i
        st[n, :, LANE:] = s.astype(BF16)
        return s * dec[n, 0:1, LANE:] + ut[n, :, LANE:]

    zero = jnp.zeros((nd, LANE), F32)
    lax.fori_loop(0, n_chunks, scan_f, zero)
    s_ctx = lax.fori_loop(0, n_ctx, functools.partial(scan_b, hi=n_ctx), zero)
    lax.fori_loop(0, n_chunks - n_ctx, functools.partial(scan_b, hi=n_chunks), s_ctx)

    lane_head = _iota((c, LANE), 1) >> 5
    vhead = _iota((c, nd), 1) >> 6
    avg = _group_avg_matrix(nd, 6)
    nw = nw_ref[...]
    ltri_w = jnp.concatenate([ltri] * NH, axis=1)
    utri_w = jnp.concatenate([utri] * NH, axis=1)

    def stack_heads(x, head_of_lane):
        return jnp.concatenate([jnp.where(head_of_lane == h, x, 0.0).astype(BF16) for h in range(NH)], axis=0)

    def p2(i, carry):
        ns = [i * CHUNK_GROUP + j for j in range(CHUNK_GROUP)]
        rows = [pl.ds(pl.multiple_of(n * c, c), c) for n in ns]
        s_fb, inter = [], []
        for n, r in zip(ns, rows):
            q = q_ref[0, r, :].astype(F32)
            k = k_ref[0, r, :].astype(F32)
            cf, cb = cf_s[r, :], cb_s[r, :]
            rf, rb = cf[c // 2 - 1:c // 2, :], cb[c // 2:c // 2 + 1, :]
            qf, qb = (q * jnp.exp(cf - rf)).astype(BF16), (q * jnp.exp(cb - rb)).astype(BF16)
            zq = jnp.zeros_like(qf)
            q_fb = jnp.concatenate([jnp.concatenate([qf, zq], axis=1), jnp.concatenate([zq, qb], axis=1)], axis=0)
            k_fb = jnp.concatenate([stack_heads(k * jnp.exp(rf - cf), lane_head),
                                    stack_heads(k * jnp.exp(rb - cb), lane_head)], axis=1)
            s_fb.append(_mm_nt(q_fb, k_fb))
            q2 = jnp.concatenate([q * jnp.exp(cf), q * jnp.exp(cb)], axis=1).astype(BF16)
            inter.append(_mm_nt(q2, st[n]))
        o = []
        for r, s_c, o_c in zip(rows, s_fb, inter):
            a = (s_c[:c] * ltri_w + s_c[c:] * utri_w).astype(BF16)
            o.append(_mm(a, stack_heads(v_ref[0, r, :].astype(F32), vhead)) + o_c)
        ms = [_group_mean(o_c * o_c, avg) for o_c in o]
        for r, o_c, ms_c in zip(rows, o, ms):
            y = o_c * lax.rsqrt(ms_c + EPS) * nw
            o_ref[0, r, :] = (_silu(g_ref[0, r, :]) * y).astype(BF16)
        return carry

    lax.fori_loop(0, n_chunks // CHUNK_GROUP, p2, 0)


def _gla(layer, q, k, v, lr, g, wgf, wgb, bg, nw):
    batch, t_all, _ = q.shape
    n_chunks = t_all // GLA_C
    nd = NH * GLA_DV
    tok = lambda n: pl.BlockSpec((1, t_all, n), lambda b: (b, 0, 0))
    par = lambda r, n: pl.BlockSpec((None, r, n), lambda b: (layer, 0, 0))
    return pl.pallas_call(
        _gla_kernel,
        grid=(batch,),
        in_specs=[tok(128), tok(128), tok(256), tok(2 * GLA_RANK), tok(256),
                  par(2 * GLA_RANK, 128), par(2 * GLA_RANK, 128), par(2, 128), par(1, 256)],
        out_specs=tok(256),
        out_shape=jax.ShapeDtypeStruct((batch, t_all, 256), BF16),
        scratch_shapes=[
            pltpu.VMEM((t_all, LANE), F32), pltpu.VMEM((t_all, LANE), F32),
            pltpu.VMEM((n_chunks, nd, 2 * LANE), F32),
            pltpu.VMEM((n_chunks, 8, 2 * LANE), F32),
            pltpu.VMEM((n_chunks, nd, 2 * LANE), BF16),
        ],
        compiler_params=pltpu.CompilerParams(
            dimension_semantics=("arbitrary",), vmem_limit_bytes=VMEM_LIMIT),
        name="gla",
    )(q, k, v, lr, g, wgf, wgb, bg, nw)


def _attn_kernel(q_ref, k_ref, v_ref, lam_ref, nw_ref, o_ref, vx_ref, *, lam_init):
    t_all = k_ref.shape[1]
    vx_ref[:, 0:DIFF_DV] = v_ref[0]
    vx_ref[:, DIFF_DV:] = jnp.ones((t_all, DIFF_DV), BF16)
    lp = lam_ref[...]
    lam = (jnp.exp(jnp.sum(lp[0:1] * lp[1:2], axis=-1, keepdims=True))
           - jnp.exp(jnp.sum(lp[2:3] * lp[3:4], axis=-1, keepdims=True)) + lam_init)
    first = _iota((TQ, LANE), 1) < DIFF_HD

    def attend(row0s, n_keys):
        k = k_ref[0, 0:n_keys, :]
        rows = [pl.ds(r, TQ) for r in row0s]
        s = []
        for r in rows:
            q = q_ref[0, r, :]
            for qh in (jnp.where(first, q, jnp.zeros_like(q)), jnp.where(first, jnp.zeros_like(q), q)):
                s.append(_mm_nt(qh, k).astype(BF16))
        e = [jnp.exp2(s_h - jnp.max(s_h, axis=-1, keepdims=True)) for s_h in s]
        ol = [_mm(e_h, vx_ref[0:n_keys, :]) for e_h in e]
        for j, r in enumerate(rows):
            o1, o2 = (x[:, 0:DIFF_DV] / x[:, DIFF_DV:] for x in ol[2 * j:2 * j + 2])
            o = o1 - lam * o2
            y = o * lax.rsqrt(jnp.mean(o * o, axis=-1, keepdims=True) + EPS) * nw_ref[...]
            o_ref[0, r, :] = (y * (1.0 - lam_init)).astype(BF16)

    attend([0], CTX)

    def group(i, carry):
        row0 = CTX + i * (ATTN_GROUP * TQ)
        attend([pl.multiple_of(row0 + j * TQ, TQ) for j in range(ATTN_GROUP)], t_all)
        return carry

    lax.fori_loop(0, (t_all - CTX) // (ATTN_GROUP * TQ), group, 0)


def _attn(layer, dq, dk, dv, diff_lambda, diff_norm_w):
    batch, t_all, _ = dq.shape
    lam_init = 0.8 - 0.6 * math.exp(-0.3 * layer)
    head = pl.BlockSpec((1, t_all, LANE), lambda b, h: (b, 0, h))
    return pl.pallas_call(
        functools.partial(_attn_kernel, lam_init=lam_init),
        grid=(batch, NH),
        in_specs=[
            head, head, head,
            pl.BlockSpec((None, 4, DIFF_HD), lambda b, h: (layer, 0, 0)),
            pl.BlockSpec((None, 1, DIFF_DV), lambda b, h: (layer, 0, 0)),
        ],
        out_specs=head,
        out_shape=jax.ShapeDtypeStruct((batch, t_all, NH * DIFF_DV), BF16),
        scratch_shapes=[pltpu.VMEM((t_all, 2 * DIFF_DV), BF16)],
        compiler_params=pltpu.CompilerParams(
            dimension_semantics=("arbitrary", "arbitrary"), vmem_limit_bytes=VMEM_LIMIT),
        name="attn",
    )(dq, dk, dv, diff_lambda, diff_norm_w)


def _post_kernel(*refs, n_stream, final, til):
    n = til.n_sub
    x_refs = refs[:n_stream]
    ro_refs, go_refs, do_refs = (refs[n_stream + i * n:n_stream + (i + 1) * n] for i in range(3))
    (g1_l, g1_c, sh_l, sh_c, sc_l, sc_c, g2_l, g2_c,
     wo_ref, nw_ref, wi_ref, wf_ref, fw_ref, o_ref) = refs[n_stream + 3 * n:]

    def rms(v, w_ref):
        return v * lax.rsqrt(jnp.mean(v * v, axis=-1, keepdims=True) + EPS) * w_ref[...]

    subs = range(n)
    att = [_mm(jnp.concatenate([ro_refs[j][0], go_refs[j][0], do_refs[j][0]], axis=-1), wo_ref[0]) for j in subs]
    xm = [til.load(x_refs, j) + til.mod(g1_l, g1_c, j) * att[j] for j in subs]
    h = [(rms(xm[j], nw_ref) * (1.0 + til.mod(sc_l, sc_c, j)) + til.mod(sh_l, sh_c, j)).astype(BF16) for j in subs]
    gu = [_mm(h[j], wi_ref[0]) for j in subs]
    ff = [_mm((_silu(g[:, :D_FF]) * g[:, D_FF:]).astype(BF16), wf_ref[0]) for g in gu]
    for j, (r0, r1) in enumerate(til.spans):
        xo = xm[j] + til.mod(g2_l, g2_c, j) * ff[j]
        o_ref[0, r0:r1, :] = rms(xo, fw_ref) if final else xo


def _post(layer, batch, t_all, stream, ro, go, do, mods, w_out, norm2_w, w_ffn_in, w_ffn_out, final_w, til, final):
    n_rows = t_all - til.s0 * til.sub
    wspec = lambda r, n: pl.BlockSpec((1, r, n), lambda b, t: (layer, 0, 0), pipeline_mode=pl.Buffered(1))
    stream_specs, stream_ops = til.in_specs(stream, D)
    tok_specs, tok_ops = zip(*(til.in_specs(a, a.shape[-1]) for a in (ro, go, do)))
    mod_specs = sum((_mod_specs(layer, j, batch) for j in (2, 3, 4, 5)), [])
    return pl.pallas_call(
        functools.partial(_post_kernel, n_stream=len(stream_ops), final=final, til=til),
        grid=(batch, n_rows // til.rows),
        in_specs=stream_specs + sum(tok_specs, []) + mod_specs + [
            wspec(D, D),
            pl.BlockSpec((None, 1, D), lambda b, t: (layer, 0, 0)),
            wspec(D, 2 * D_FF), wspec(D_FF, D),
            pl.BlockSpec((1, D), lambda b, t: (0, 0)),
        ],
        out_specs=pl.BlockSpec((1, til.rows, D), lambda b, t: (b, t, 0)),
        out_shape=jax.ShapeDtypeStruct((batch, n_rows, D), F32),
        compiler_params=pltpu.CompilerParams(
            dimension_semantics=("arbitrary", "arbitrary"), vmem_limit_bytes=VMEM_LIMIT),
        name="post",
    )(*stream_ops, *sum(tok_ops, []), *([mods] * 8), w_out, norm2_w, w_ffn_in, w_ffn_out, final_w)


def _rope_tables(n_lat):
    lane = np.arange(LANE)
    first = ((lane % 32) < 16)[None, :]

    def pack(ang):
        cos, sin = np.cos(ang), np.sin(ang)
        tabs = (cos, np.where(first, -sin, 0.0), np.where(first, 0.0, sin))
        ident = (np.ones((CTX, LANE)), np.zeros((CTX, LANE)), np.zeros((CTX, LANE)))
        return tuple(jnp.asarray(np.concatenate([i, t], axis=0), dtype=F32) for i, t in zip(ident, tabs))

    idx = np.arange(n_lat, dtype=np.float64)
    ret_freq = 1.0 / (ROPE_BASE ** np.linspace(0.0, 1.0, RET_DK // 2))
    ret = pack((idx[:, None] * ret_freq[None, :])[:, lane % 16])
    ax_freq = 1.0 / (ROPE_BASE ** (np.arange(DIFF_HD // 4) / (DIFF_HD // 4)))
    row_ang = np.floor(idx / GRID_W)[:, None] * ax_freq[None, :]
    col_ang = (idx % GRID_W)[:, None] * ax_freq[None, :]
    is_row = ((lane % 64) < 32)[None, :]
    diff = pack(np.where(is_row, row_ang[:, lane % 16], col_ang[:, lane % 16]))
    return ret + diff


def kernel(x, c, ctx, c_ctx, w_ada, b_ada, norm1_w, w_in, ret_decay_logit, gla_w_gate, gla_b_gate,
           gla_norm_w, diff_lambda, diff_norm_w, w_out, norm2_w, w_ffn_in, w_ffn_out, final_norm_w):
    batch, n_lat, _ = x.shape
    assert ctx.shape[1] == CTX and batch < 16

    w_in_a = w_in[..., :N_PROJ_A].astype(BF16)
    w_in_d = w_in[..., N_PROJ_A:].astype(BF16)
    w_out_b = w_out.astype(BF16)
    w_ffn_in_b = w_ffn_in.astype(BF16)
    w_ffn_out_b = w_ffn_out.astype(BF16)
    zpad = jnp.zeros((DEPTH, GLA_RANK, NH * GLA_DK), F32)
    wgf = jnp.concatenate([gla_w_gate[:, 0], zpad], axis=1).astype(BF16)
    wgb = jnp.concatenate([zpad, gla_w_gate[:, 1]], axis=1).astype(BF16)
    gla_nw = jnp.tile(gla_norm_w, (1, NH)).reshape(DEPTH, 1, NH * GLA_DV)
    norm1 = norm1_w.reshape(DEPTH, 1, D)
    norm2 = norm2_w.reshape(DEPTH, 1, D)
    diff_nw = diff_norm_w.reshape(DEPTH, 1, DIFF_DV)
    final_w = final_norm_w.reshape(1, D)
    tabs = _rope_tables(n_lat)

    cond = jnp.concatenate([c, c_ctx[None, :], jnp.zeros((16 - batch - 1, D), F32)], axis=0)
    mods = _ada(cond, w_ada, b_ada).reshape(DEPTH * 16 * 6, 1, D)

    t_all = CTX + n_lat
    stream = (ctx, x)
    for layer in range(DEPTH):
        final = layer == DEPTH - 1
        til = _Tiling(*POST_TILING["first" if layer == 0 else "last" if final else "middle"])
        rq, rk, rv, rg, gq, gk, gv, gr, dq, dk, dv, lr = _proj(
            layer, batch, t_all, stream, mods, norm1, w_in_a, w_in_d, tabs)
        ro = _ret(layer, rq, rk, rv, rg, ret_decay_logit)
        go = _gla(layer, gq, gk, gv, lr, gr, wgf, wgb, gla_b_gate, gla_nw)
        do = _attn(layer, dq, dk, dv, diff_lambda, diff_nw)
        stream = _post(layer, batch, t_all, stream, ro, go, do, mods, w_out_b, norm2, w_ffn_in_b, w_ffn_out_b,
                       final_w, til, final)
    return stream
```

```python
import functools
import math

import jax
import jax.numpy as jnp
import numpy as np
from jax import lax
from jax.experimental import pallas as pl
from jax.experimental.pallas import tpu as pltpu

F32 = jnp.float32
BF16 = jnp.bfloat16

D = 1024
DEPTH = 4
CTX = 256
GRID_W = 64
EPS = 1e-6
GN_EPS = 1e-5
ROPE_BASE = 10000.0

NH = 4
RET_DK, RET_DV = 32, 64
GLA_DK, GLA_DV = 32, 64
GLA_RANK = 16
GLA_TAU = 16.0
DIFF_HD, DIFF_DV = 64, 128
D_FF = 2816

O_RQ, O_RK, O_RV, O_RG = 0, 128, 256, 512
O_GQ, O_GK, O_GV, O_GR = 768, 896, 1024, 1280
O_LR = 1536
N_PROJ_A = O_LR + 2 * GLA_RANK

LANE = 128
PROJ_SUBTILES = 3
POST_TILING = {"first": (256, 3, 0), "middle": (288, 2, 0), "last": (256, 2, 1)}
TQ = 256
RET_C = 128
GLA_C = 128
CHUNK_UNROLL = 6
CHUNK_GROUP = 6
VMEM_LIMIT = 56 * 1024 * 1024


def _mm(a, b):
    return jnp.dot(a, b, preferred_element_type=F32)


def _mm_nt(a, b):
    return lax.dot_general(a, b, (((1,), (1,)), ((), ())), preferred_element_type=F32)


def _mm_tn(a, b):
    return lax.dot_general(a, b, (((0,), (0,)), ((), ())), preferred_element_type=F32)


def _split(x):
    hi = x.astype(BF16)
    lo = (x - hi.astype(F32)).astype(BF16)
    return hi, lo


def _hilo_mm(m_bf16, x):
    hi, lo = _split(x)
    n = x.shape[1]
    r = _mm(m_bf16, jnp.concatenate([hi, lo], axis=1))
    return r[:, :n] + r[:, n:]


def _group_mean(x, avg_bf16):
    hi, lo = _split(x)
    m = x.shape[0]
    r = _mm(jnp.concatenate([hi, lo], axis=0), avg_bf16)
    return r[:m] + r[m:]


def _log_sigmoid(z):
    return jnp.minimum(z, 0.0) - jnp.log(1.0 + jnp.exp(-jnp.abs(z)))


def _silu(z):
    return z / (1.0 + jnp.exp(-z))


def _iota(shape, dim):
    return lax.broadcasted_iota(jnp.int32, shape, dim)


def _group_avg_matrix(n, group_shift):
    r = _iota((n, n), 0) >> group_shift
    c = _iota((n, n), 1) >> group_shift
    return jnp.where(r == c, 1.0 / (1 << group_shift), 0.0).astype(BF16)


def _ada_kernel(cond_ref, w_ref, b_ref, o_ref):
    cond = _silu(cond_ref[...])
    w = w_ref[0]
    c_hi, c_lo = _split(cond)
    w_hi, w_lo = _split(w)
    o_ref[0] = _mm(c_hi, w_hi) + _mm(c_hi, w_lo) + _mm(c_lo, w_hi) + b_ref[0]


def _ada(cond, w_ada, b_ada):
    bn = 1536
    n = w_ada.shape[-1]
    return pl.pallas_call(
        _ada_kernel,
        grid=(DEPTH, n // bn),
        in_specs=[
            pl.BlockSpec((16, D), lambda l, j: (0, 0)),
            pl.BlockSpec((1, D, bn), lambda l, j: (l, 0, j)),
            pl.BlockSpec((1, 1, bn), lambda l, j: (l, 0, j)),
        ],
        out_specs=pl.BlockSpec((1, 16, bn), lambda l, j: (l, 0, j)),
        out_shape=jax.ShapeDtypeStruct((DEPTH, 16, n), F32),
        compiler_params=pltpu.CompilerParams(dimension_semantics=("arbitrary", "arbitrary")),
        name="ada",
    )(cond, w_ada, b_ada.reshape(DEPTH, 1, n))


def _split_w_in_kernel(w_ref, a_ref, d_ref):
    w = w_ref[0]
    a_ref[0] = w[:, :N_PROJ_A].astype(BF16)
    d_ref[0] = w[:, N_PROJ_A:].astype(BF16)


def _split_w_in(w_in):
    depth, d_in, n = w_in.shape
    rows = 512
    return pl.pallas_call(
        _split_w_in_kernel,
        grid=(depth, d_in // rows),
        in_specs=[pl.BlockSpec((1, rows, n), lambda l, r: (l, r, 0))],
        out_specs=[pl.BlockSpec((1, rows, N_PROJ_A), lambda l, r: (l, r, 0)),
                   pl.BlockSpec((1, rows, n - N_PROJ_A), lambda l, r: (l, r, 0))],
        out_shape=[jax.ShapeDtypeStruct((depth, d_in, N_PROJ_A), BF16),
                   jax.ShapeDtypeStruct((depth, d_in, n - N_PROJ_A), BF16)],
        compiler_params=pltpu.CompilerParams(dimension_semantics=("arbitrary", "arbitrary")),
        name="split_w_in",
    )(w_in)


def _mod_specs(layer, j, batch):
    lat = pl.BlockSpec((None, 1, D), lambda b, t: ((layer * 16 + b) * 6 + j, 0, 0))
    ctx = pl.BlockSpec((None, 1, D), lambda b, t: ((layer * 16 + batch) * 6 + j, 0, 0))
    return [lat, ctx]


class _Tiling:
    def __init__(self, sub, n_sub, s0=0):
        self.sub, self.n_sub, self.s0 = sub, n_sub, s0
        self.rows = sub * n_sub
        self.spans = [(j * sub, (j + 1) * sub) for j in range(n_sub)]

    def s(self, j, t=None):
        return (pl.program_id(1) if t is None else t) * self.n_sub + j + self.s0

    def in_specs(self, arr, width):
        specs, ops = [], []
        for j in range(self.n_sub):
            if isinstance(arr, tuple):
                assert CTX % self.sub == 0
                n_ctx = CTX // self.sub
                specs += [
                    pl.BlockSpec((1, self.sub, width), lambda b, t, j=j: (b, jnp.minimum(self.s(j, t), n_ctx - 1), 0)),
                    pl.BlockSpec((1, self.sub, width), lambda b, t, j=j: (b, jnp.maximum(self.s(j, t) - n_ctx, 0), 0))]
                ops += list(arr)
            else:
                specs.append(pl.BlockSpec((1, self.sub, width), lambda b, t, j=j: (b, self.s(j, t), 0)))
                ops.append(arr)
        return specs, ops

    def load(self, refs, j):
        if len(refs) == 2 * self.n_sub:
            return jnp.where(self.s(j) < CTX // self.sub, refs[2 * j][0], refs[2 * j + 1][0])
        return refs[j][0]

    def mod(self, lat_ref, ctx_ref, j):
        if CTX % self.sub == 0:
            return jnp.where(self.s(j) < CTX // self.sub, ctx_ref[...], lat_ref[...])
        assert self.sub > CTX and self.s0 == 0
        if j > 0:
            return lat_ref[...]
        ctx_rows = (_iota((self.sub, 1), 0) < CTX) & (pl.program_id(1) == 0)
        return jnp.where(ctx_rows, ctx_ref[...], lat_ref[...])


def _rope(x, cos, sa, sb):
    outs = []
    for j in range(x.shape[1] // LANE):
        blk = x[:, j * LANE:(j + 1) * LANE]
        fwd = pltpu.roll(blk, LANE - 16, 1)
        bwd = pltpu.roll(blk, 16, 1)
        outs.append(blk * cos + fwd * sa + bwd * sb)
    return outs[0] if len(outs) == 1 else jnp.concatenate(outs, axis=1)


def _proj_kernel(*refs, n_stream, til):
    (sh_l, sh_c, sc_l, sc_c, nw_ref, wa_ref, wd_ref, rcos, rsa, rsb, dcos, dsa, dsb,
     rq_o, rk_o, rv_o, rg_o, gq_o, gk_o, gv_o, gr_o, dq_o, dk_o, dv_o, lr_o) = refs[n_stream:]
    h = []
    for j in range(til.n_sub):
        xj = til.load(refs[:n_stream], j)
        y = xj * lax.rsqrt(jnp.mean(xj * xj, axis=-1, keepdims=True) + EPS) * nw_ref[...]
        h.append((y * (1.0 + til.mod(sc_l, sc_c, j)) + til.mod(sh_l, sh_c, j)).astype(BF16))
    p = [(_mm(hj, wa_ref[0]), _mm(hj, wd_ref[0])) for hj in h]

    avg = _group_avg_matrix(NH * RET_DV, 6)
    for (r0, r1), (pa, pd) in zip(til.spans, p):
        rc, ra, rb = rcos[r0:r1, :], rsa[r0:r1, :], rsb[r0:r1, :]
        rq_o[0, r0:r1, :] = _rope(pa[:, O_RQ:O_RQ + 128], rc, ra, rb).astype(BF16)
        rk_o[0, r0:r1, :] = (_rope(pa[:, O_RK:O_RK + 128], rc, ra, rb) * (RET_DK ** -0.5)).astype(BF16)
        rv = pa[:, O_RV:O_RV + 256]
        rv_o[0, r0:r1, :] = (rv - _group_mean(rv, avg)).astype(BF16)
        rg_o[0, r0:r1, :] = pa[:, O_RG:O_RG + 256]
        gq_o[0, r0:r1, :] = (pa[:, O_GQ:O_GQ + 128] * (GLA_DK ** -0.5)).astype(BF16)
        gk_o[0, r0:r1, :] = pa[:, O_GK:O_GK + 128].astype(BF16)
        gv_o[0, r0:r1, :] = pa[:, O_GV:O_GV + 256].astype(BF16)
        gr_o[0, r0:r1, :] = pa[:, O_GR:O_GR + 256]
        lr_o[0, r0:r1, :] = pa[:, O_LR:O_LR + 2 * GLA_RANK].astype(BF16)
        dc, da, db = dcos[r0:r1, :], dsa[r0:r1, :], dsb[r0:r1, :]
        dq_o[0, r0:r1, :] = (_rope(pd[:, 0:512], dc, da, db)
                             * (DIFF_HD ** -0.5 * math.log2(math.e))).astype(BF16)
        dk_o[0, r0:r1, :] = _rope(pd[:, 512:1024], dc, da, db).astype(BF16)
        dv_o[0, r0:r1, :] = pd[:, 1024:1536].astype(BF16)


def _proj(layer, batch, t_all, stream, mods, norm1_w, w_a, w_d, tabs):
    til = _Tiling(CTX, PROJ_SUBTILES)
    tm = til.rows
    tok = lambda n: pl.BlockSpec((1, tm, n), lambda b, t: (b, t, 0))
    tab = pl.BlockSpec((tm, LANE), lambda b, t: (t, 0))
    widths = [128, 128, 256, 256, 128, 128, 256, 256, 512, 512, 512, 2 * GLA_RANK]
    dtypes = [BF16, BF16, BF16, F32, BF16, BF16, BF16, F32, BF16, BF16, BF16, BF16]
    stream_specs, stream_ops = til.in_specs(stream, D)
    wspec = lambda n: pl.BlockSpec((1, D, n), lambda b, t: (layer, 0, 0), pipeline_mode=pl.Buffered(1))
    return pl.pallas_call(
        functools.partial(_proj_kernel, n_stream=len(stream_ops), til=til),
        grid=(batch, t_all // tm),
        in_specs=stream_specs + _mod_specs(layer, 0, batch) + _mod_specs(layer, 1, batch) + [
            pl.BlockSpec((None, 1, D), lambda b, t: (layer, 0, 0)),
            wspec(w_a.shape[-1]), wspec(w_d.shape[-1]),
            tab, tab, tab, tab, tab, tab,
        ],
        out_specs=[tok(n) for n in widths],
        out_shape=[jax.ShapeDtypeStruct((batch, t_all, n), dt) for n, dt in zip(widths, dtypes)],
        compiler_params=pltpu.CompilerParams(
            dimension_semantics=("arbitrary", "arbitrary"), vmem_limit_bytes=VMEM_LIMIT),
        name="proj",
    )(*stream_ops, mods, mods, mods, mods, norm1_w, w_a, w_d, *tabs)


def _ret_kernel(logit_ref, q_ref, k_ref, v_ref, g_ref, o_ref, ut, st):
    n_chunks = q_ref.shape[1] // RET_C
    n_ctx = CTX // RET_C
    c = RET_C

    lane_head = _iota((c, LANE), 1) >> 5
    i_f = _iota((c, LANE), 0).astype(F32)

    def lanes_of(d):
        z = jnp.zeros((c, LANE), F32)
        for h in range(NH):
            z = jnp.where(lane_head == h, logit_ref[d, h], z)
        return _log_sigmoid(z)

    lg_f, lg_b = lanes_of(0), lanes_of(1)
    kdec_f = jnp.exp((c - 1.0 - i_f) * lg_f)
    kdec_b = jnp.exp(i_f * lg_b)
    qdec_f = jnp.exp((i_f + 1.0) * lg_f)
    qdec_b = jnp.exp((c - i_f) * lg_b)
    cd_f = jnp.exp(c * lg_f[0:1])
    cd_b = jnp.exp(c * lg_b[0:1])
    kdec = jnp.concatenate([kdec_f, kdec_b], axis=1)
    qdec = jnp.concatenate([qdec_f, qdec_b], axis=1)

    nd = NH * RET_DV
    bd = (_iota((nd, 2 * LANE), 0) >> 6) == ((_iota((nd, 2 * LANE), 1) & (LANE - 1)) >> 5)

    def p0(n, carry):
        rows = pl.ds(pl.multiple_of(n * c, c), c)
        k = k_ref[0, rows, :].astype(F32)
        k2 = (jnp.concatenate([k, k], axis=1) * kdec).astype(BF16)
        ut[n] = jnp.where(bd, _mm_tn(v_ref[0, rows, :], k2), 0.0)
        return carry

    lax.fori_loop(0, n_chunks, p0, 0, unroll=CHUNK_UNROLL)

    s = jnp.zeros((nd, LANE), F32)
    for n in range(n_chunks):
        st[n, :, 0:LANE] = s.astype(BF16)
        s = s * cd_f + ut[n, :, 0:LANE]
    s = jnp.zeros((nd, LANE), F32)
    for n in list(range(n_ctx - 1, -1, -1)) + list(range(n_chunks - 1, n_ctx - 1, -1)):
        st[n, :, LANE:] = s.astype(BF16)
        s = s * cd_b + ut[n, :, LANE:]

    rel = (_iota((c, c), 0) - _iota((c, c), 1)).astype(F32)
    eye = jnp.where(rel == 0.0, 1.0, 0.0)
    dmats = []
    for h in range(NH):
        lf = _log_sigmoid(jnp.full((c, c), logit_ref[0, h], F32))
        lb = _log_sigmoid(jnp.full((c, c), logit_ref[1, h], F32))
        dmats.append(jnp.exp(jnp.where(rel >= 0.0, rel * lf, -rel * lb)) + eye)
    dwide = jnp.concatenate(dmats, axis=1)
    vhead = _iota((c, NH * RET_DV), 1) >> 6
    avg = _group_avg_matrix(NH * RET_DV, 6)

    def p2(i, carry):
        ns = [i * CHUNK_GROUP + j for j in range(CHUNK_GROUP)]
        rows = [pl.ds(pl.multiple_of(n * c, c), c) for n in ns]
        sc, inter = [], []
        for n, r in zip(ns, rows):
            q = q_ref[0, r, :]
            k = k_ref[0, r, :].astype(F32)
            kbd = jnp.concatenate([jnp.where(lane_head == h, k, 0.0).astype(BF16) for h in range(NH)], axis=0)
            sc.append(_mm_nt(q, kbd))
            qf = q.astype(F32)
            inter.append(_mm_nt((jnp.concatenate([qf, qf], axis=1) * qdec).astype(BF16), st[n]))
        o = []
        for r, s_c, o_c in zip(rows, sc, inter):
            v = v_ref[0, r, :].astype(F32)
            vbd = jnp.concatenate([jnp.where(vhead == h, v, 0.0).astype(BF16) for h in range(NH)], axis=0)
            o.append(_mm((s_c * dwide).astype(BF16), vbd) + o_c)
        var = [_mm((o_c * o_c).astype(BF16), avg) for o_c in o]
        for r, o_c, v_c in zip(rows, o, var):
            o_ref[0, r, :] = (_silu(g_ref[0, r, :]) * o_c * lax.rsqrt(v_c + GN_EPS)).astype(BF16)
        return carry

    lax.fori_loop(0, n_chunks // CHUNK_GROUP, p2, 0)


def _ret(layer, q, k, v, g, ret_decay_logit):
    batch, t_all, _ = q.shape
    n_chunks = t_all // RET_C
    tok = lambda n: pl.BlockSpec((1, t_all, n), lambda b, lg: (b, 0, 0))
    return pl.pallas_call(
        _ret_kernel,
        grid_spec=pltpu.PrefetchScalarGridSpec(
            num_scalar_prefetch=1,
            grid=(batch,),
            in_specs=[tok(128), tok(128), tok(256), tok(256)],
            out_specs=tok(256),
            scratch_shapes=[
                pltpu.VMEM((n_chunks, NH * RET_DV, 2 * LANE), F32),
                pltpu.VMEM((n_chunks, NH * RET_DV, 2 * LANE), BF16),
            ],
        ),
        out_shape=jax.ShapeDtypeStruct((batch, t_all, 256), BF16),
        compiler_params=pltpu.CompilerParams(
            dimension_semantics=("arbitrary",), vmem_limit_bytes=VMEM_LIMIT),
        name="ret",
    )(ret_decay_logit[layer], q, k, v, g)


def _gla_kernel(q_ref, k_ref, v_ref, lr_ref, g_ref, wgf_ref, wgb_ref, bg_ref, nw_ref, o_ref,
                cf_s, cb_s, ut, dec, st):
    c = GLA_C
    n_chunks = q_ref.shape[1] // c
    n_ctx = CTX // c
    nd = NH * GLA_DV

    ltri = jnp.where(_iota((c, c), 0) >= _iota((c, c), 1), 1.0, 0.0)
    utri = jnp.where(_iota((c, c), 0) <= _iota((c, c), 1), 1.0, 0.0)
    ltri_b, utri_b = ltri.astype(BF16), utri.astype(BF16)
    bd = (_iota((nd, 2 * LANE), 0) >> 6) == ((_iota((nd, 2 * LANE), 1) & (LANE - 1)) >> 5)

    lr = lr_ref[0]
    cf_s[...] = _log_sigmoid(_mm(lr, wgf_ref[...]) + bg_ref[0:1, :]) * (1.0 / GLA_TAU)
    cb_s[...] = _log_sigmoid(_mm(lr, wgb_ref[...]) + bg_ref[1:2, :]) * (1.0 / GLA_TAU)

    def p0(i, carry):
        ns = [i * CHUNK_GROUP + j for j in range(CHUNK_GROUP)]
        rows = [pl.ds(pl.multiple_of(n * c, c), c) for n in ns]
        cf = [_hilo_mm(ltri_b, cf_s[r, :]) for r in rows]
        cb = [_hilo_mm(utri_b, cb_s[r, :]) for r in rows]
        for n, r, cf_c, cb_c in zip(ns, rows, cf, cb):
            cf_s[r, :] = cf_c
            cb_s[r, :] = cb_c
            cf_end, cb_end = cf_c[c - 1:c, :], cb_c[0:1, :]
            k = k_ref[0, r, :].astype(F32)
            k2 = jnp.concatenate([k * jnp.exp(cf_end - cf_c), k * jnp.exp(cb_end - cb_c)], axis=1).astype(BF16)
            ut[n] = jnp.where(bd, _mm_tn(v_ref[0, r, :], k2), 0.0)
            dec[n] = jnp.broadcast_to(jnp.exp(jnp.concatenate([cf_end, cb_end], axis=1)), (8, 2 * LANE))
        return carry

    lax.fori_loop(0, n_chunks // CHUNK_GROUP, p0, 0)

    def scan_f(n, s):
        st[n, :, 0:LANE] = s.astype(BF16)
        return s * dec[n, 0:1, 0:LANE] + ut[n, :, 0:LANE]

    def scan_b(i, s, hi):
        n = hi - 1 - i
        st[n, :, LANE:] = s.astype(BF16)
        return s * dec[n, 0:1, LANE:] + ut[n, :, LANE:]

    zero = jnp.zeros((nd, LANE), F32)
    lax.fori_loop(0, n_chunks, scan_f, zero)
    s_ctx = lax.fori_loop(0, n_ctx, functools.partial(scan_b, hi=n_ctx), zero)
    lax.fori_loop(0, n_chunks - n_ctx, functools.partial(scan_b, hi=n_chunks), s_ctx)

    lane_head = _iota((c, LANE), 1) >> 5
    vhead = _iota((c, nd), 1) >> 6
    avg = _group_avg_matrix(nd, 6)
    nw = nw_ref[...]
    ltri_w = jnp.concatenate([ltri] * NH, axis=1)
    utri_w = jnp.concatenate([utri] * NH, axis=1)

    def stack_heads(x, head_of_lane):
        return jnp.concatenate([jnp.where(head_of_lane == h, x, 0.0).astype(BF16) for h in range(NH)], axis=0)

    def p2(i, carry):
        ns = [i * CHUNK_GROUP + j for j in range(CHUNK_GROUP)]
        rows = [pl.ds(pl.multiple_of(n * c, c), c) for n in ns]
        s_fb, inter = [], []
        for n, r in zip(ns, rows):
            q = q_ref[0, r, :].astype(F32)
            k = k_ref[0, r, :].astype(F32)
            cf, cb = cf_s[r, :], cb_s[r, :]
            rf, rb = cf[c // 2 - 1:c // 2, :], cb[c // 2:c // 2 + 1, :]
            qf, qb = (q * jnp.exp(cf - rf)).astype(BF16), (q * jnp.exp(cb - rb)).astype(BF16)
            zq = jnp.zeros_like(qf)
            q_fb = jnp.concatenate([jnp.concatenate([qf, zq], axis=1), jnp.concatenate([zq, qb], axis=1)], axis=0)
            k_fb = jnp.concatenate([stack_heads(k * jnp.exp(rf - cf), lane_head),
                                    stack_heads(k * jnp.exp(rb - cb), lane_head)], axis=1)
            s_fb.append(_mm_nt(q_fb, k_fb))
            q2 = jnp.concatenate([q * jnp.exp(cf), q * jnp.exp(cb)], axis=1).astype(BF16)
            inter.append(_mm_nt(q2, st[n]))
        o = []
        for r, s_c, o_c in zip(rows, s_fb, inter):
            a = (s_c[:c] * ltri_w + s_c[c:] * utri_w).astype(BF16)
            o.append(_mm(a, stack_heads(v_ref[0, r, :].astype(F32), vhead)) + o_c)
        ms = [_mm((o_c * o_c).astype(BF16), avg) for o_c in o]
        for r, o_c, ms_c in zip(rows, o, ms):
            y = o_c * lax.rsqrt(ms_c + EPS) * nw
            o_ref[0, r, :] = (_silu(g_ref[0, r, :]) * y).astype(BF16)
        return carry

    lax.fori_loop(0, n_chunks // CHUNK_GROUP, p2, 0)


def _gla(layer, q, k, v, lr, g, wgf, wgb, bg, nw):
    batch, t_all, _ = q.shape
    n_chunks = t_all // GLA_C
    nd = NH * GLA_DV
    tok = lambda n: pl.BlockSpec((1, t_all, n), lambda b: (b, 0, 0))
    par = lambda r, n: pl.BlockSpec((None, r, n), lambda b: (layer, 0, 0))
    return pl.pallas_call(
        _gla_kernel,
        grid=(batch,),
        in_specs=[tok(128), tok(128), tok(256), tok(2 * GLA_RANK), tok(256),
                  par(2 * GLA_RANK, 128), par(2 * GLA_RANK, 128), par(2, 128), par(1, 256)],
        out_specs=tok(256),
        out_shape=jax.ShapeDtypeStruct((batch, t_all, 256), BF16),
        scratch_shapes=[
            pltpu.VMEM((t_all, LANE), F32), pltpu.VMEM((t_all, LANE), F32),
            pltpu.VMEM((n_chunks, nd, 2 * LANE), F32),
            pltpu.VMEM((n_chunks, 8, 2 * LANE), F32),
            pltpu.VMEM((n_chunks, nd, 2 * LANE), BF16),
        ],
        compiler_params=pltpu.CompilerParams(
            dimension_semantics=("arbitrary",), vmem_limit_bytes=VMEM_LIMIT),
        name="gla",
    )(q, k, v, lr, g, wgf, wgb, bg, nw)


def _attn_kernel(q_ref, k_ref, v_ref, lam_ref, nw_ref, o_ref, vx_ref, *, lam_init):
    t_all = k_ref.shape[1]
    vx_ref[:, 0:DIFF_DV] = v_ref[0]
    vx_ref[:, DIFF_DV:] = jnp.ones((t_all, DIFF_DV), BF16)
    lp = lam_ref[...]
    lam = (jnp.exp(jnp.sum(lp[0:1] * lp[1:2], axis=-1, keepdims=True))
           - jnp.exp(jnp.sum(lp[2:3] * lp[3:4], axis=-1, keepdims=True)) + lam_init)
    first = _iota((TQ, LANE), 1) < DIFF_HD

    assert CTX == TQ
    tiles = [(r, CTX if r == 0 else t_all) for r in range(0, t_all, TQ)]
    s = []
    for r, n_keys in tiles:
        q = q_ref[0, r:r + TQ, :]
        for qh in (jnp.where(first, q, jnp.zeros_like(q)), jnp.where(first, jnp.zeros_like(q), q)):
            s.append(_mm_nt(qh, k_ref[0, 0:n_keys, :]).astype(BF16))
    e = [jnp.exp2(s_h - jnp.max(s_h, axis=-1, keepdims=True)) for s_h in s]
    ol = [_mm(e_h, vx_ref[0:e_h.shape[1], :]) for e_h in e]
    for j, (r, _) in enumerate(tiles):
        o1, o2 = (x[:, 0:DIFF_DV] / x[:, DIFF_DV:] for x in ol[2 * j:2 * j + 2])
        o = o1 - lam * o2
        y = o * lax.rsqrt(jnp.mean(o * o, axis=-1, keepdims=True) + EPS) * nw_ref[...]
        o_ref[0, r:r + TQ, :] = (y * (1.0 - lam_init)).astype(BF16)


def _attn(layer, dq, dk, dv, diff_lambda, diff_norm_w):
    batch, t_all, _ = dq.shape
    lam_init = 0.8 - 0.6 * math.exp(-0.3 * layer)
    head = pl.BlockSpec((1, t_all, LANE), lambda b, h: (b, 0, h))
    return pl.pallas_call(
        functools.partial(_attn_kernel, lam_init=lam_init),
        grid=(batch, NH),
        in_specs=[
            head, head, head,
            pl.BlockSpec((None, 4, DIFF_HD), lambda b, h: (layer, 0, 0)),
            pl.BlockSpec((None, 1, DIFF_DV), lambda b, h: (layer, 0, 0)),
        ],
        out_specs=head,
        out_shape=jax.ShapeDtypeStruct((batch, t_all, NH * DIFF_DV), BF16),
        scratch_shapes=[pltpu.VMEM((t_all, 2 * DIFF_DV), BF16)],
        compiler_params=pltpu.CompilerParams(
            dimension_semantics=("arbitrary", "arbitrary"), vmem_limit_bytes=VMEM_LIMIT),
        name="attn",
    )(dq, dk, dv, diff_lambda, diff_norm_w)


def _post_kernel(*refs, n_stream, final, til):
    n = til.n_sub
    x_refs = refs[:n_stream]
    ro_refs, go_refs, do_refs = (refs[n_stream + i * n:n_stream + (i + 1) * n] for i in range(3))
    (g1_l, g1_c, sh_l, sh_c, sc_l, sc_c, g2_l, g2_c,
     wo_ref, nw_ref, wi_ref, wf_ref, fw_ref, o_ref) = refs[n_stream + 3 * n:]

    def rms(v, w_ref):
        return v * lax.rsqrt(jnp.mean(v * v, axis=-1, keepdims=True) + EPS) * w_ref[...]

    subs = range(n)
    att = [_mm(jnp.concatenate([ro_refs[j][0], go_refs[j][0], do_refs[j][0]], axis=-1), wo_ref[0]) for j in subs]
    xm = [til.load(x_refs, j) + til.mod(g1_l, g1_c, j) * att[j] for j in subs]
    h = [(rms(xm[j], nw_ref) * (1.0 + til.mod(sc_l, sc_c, j)) + til.mod(sh_l, sh_c, j)).astype(BF16) for j in subs]
    gu = [_mm(h[j], wi_ref[0]) for j in subs]
    ff = [_mm((_silu(g[:, :D_FF]) * g[:, D_FF:]).astype(BF16), wf_ref[0]) for g in gu]
    for j, (r0, r1) in enumerate(til.spans):
        xo = xm[j] + til.mod(g2_l, g2_c, j) * ff[j]
        o_ref[0, r0:r1, :] = rms(xo, fw_ref) if final else xo


def _post(layer, batch, t_all, stream, ro, go, do, mods, w_out, norm2_w, w_ffn_in, w_ffn_out, final_w, til, final):
    n_rows = t_all - til.s0 * til.sub
    wspec = lambda r, n: pl.BlockSpec((1, r, n), lambda b, t: (layer, 0, 0), pipeline_mode=pl.Buffered(1))
    stream_specs, stream_ops = til.in_specs(stream, D)
    tok_specs, tok_ops = zip(*(til.in_specs(a, a.shape[-1]) for a in (ro, go, do)))
    mod_specs = sum((_mod_specs(layer, j, batch) for j in (2, 3, 4, 5)), [])
    return pl.pallas_call(
        functools.partial(_post_kernel, n_stream=len(stream_ops), final=final, til=til),
        grid=(batch, n_rows // til.rows),
        in_specs=stream_specs + sum(tok_specs, []) + mod_specs + [
            wspec(D, D),
            pl.BlockSpec((None, 1, D), lambda b, t: (layer, 0, 0)),
            wspec(D, 2 * D_FF), wspec(D_FF, D),
            pl.BlockSpec((1, D), lambda b, t: (0, 0)),
        ],
        out_specs=pl.BlockSpec((1, til.rows, D), lambda b, t: (b, t, 0)),
        out_shape=jax.ShapeDtypeStruct((batch, n_rows, D), F32),
        compiler_params=pltpu.CompilerParams(
            dimension_semantics=("arbitrary", "arbitrary"), vmem_limit_bytes=VMEM_LIMIT),
        name="post",
    )(*stream_ops, *sum(tok_ops, []), *([mods] * 8), w_out, norm2_w, w_ffn_in, w_ffn_out, final_w)


def _rope_tables(n_lat):
    lane = np.arange(LANE)
    first = ((lane % 32) < 16)[None, :]

    def pack(ang):
        cos, sin = np.cos(ang), np.sin(ang)
        tabs = (cos, np.where(first, -sin, 0.0), np.where(first, 0.0, sin))
        ident = (np.ones((CTX, LANE)), np.zeros((CTX, LANE)), np.zeros((CTX, LANE)))
        return tuple(jnp.asarray(np.concatenate([i, t], axis=0), dtype=F32) for i, t in zip(ident, tabs))

    idx = np.arange(n_lat, dtype=np.float64)
    ret_freq = 1.0 / (ROPE_BASE ** np.linspace(0.0, 1.0, RET_DK // 2))
    ret = pack((idx[:, None] * ret_freq[None, :])[:, lane % 16])
    ax_freq = 1.0 / (ROPE_BASE ** (np.arange(DIFF_HD // 4) / (DIFF_HD // 4)))
    row_ang = np.floor(idx / GRID_W)[:, None] * ax_freq[None, :]
    col_ang = (idx % GRID_W)[:, None] * ax_freq[None, :]
    is_row = ((lane % 64) < 32)[None, :]
    diff = pack(np.where(is_row, row_ang[:, lane % 16], col_ang[:, lane % 16]))
    return ret + diff


def kernel(x, c, ctx, c_ctx, w_ada, b_ada, norm1_w, w_in, ret_decay_logit, gla_w_gate, gla_b_gate,
           gla_norm_w, diff_lambda, diff_norm_w, w_out, norm2_w, w_ffn_in, w_ffn_out, final_norm_w):
    batch, n_lat, _ = x.shape
    assert ctx.shape[1] == CTX and batch < 16

    w_in_a, w_in_d = _split_w_in(w_in)
    w_out_b = w_out.astype(BF16)
    w_ffn_in_b = w_ffn_in.astype(BF16)
    w_ffn_out_b = w_ffn_out.astype(BF16)
    zpad = jnp.zeros((DEPTH, GLA_RANK, NH * GLA_DK), F32)
    wgf = jnp.concatenate([gla_w_gate[:, 0], zpad], axis=1).astype(BF16)
    wgb = jnp.concatenate([zpad, gla_w_gate[:, 1]], axis=1).astype(BF16)
    gla_nw = jnp.tile(gla_norm_w, (1, NH)).reshape(DEPTH, 1, NH * GLA_DV)
    norm1 = norm1_w.reshape(DEPTH, 1, D)
    norm2 = norm2_w.reshape(DEPTH, 1, D)
    diff_nw = diff_norm_w.reshape(DEPTH, 1, DIFF_DV)
    final_w = final_norm_w.reshape(1, D)
    tabs = _rope_tables(n_lat)

    cond = jnp.concatenate([c, c_ctx[None, :], jnp.zeros((16 - batch - 1, D), F32)], axis=0)
    mods = _ada(cond, w_ada, b_ada).reshape(DEPTH * 16 * 6, 1, D)

    t_all = CTX + n_lat
    stream = (ctx, x)
    for layer in range(DEPTH):
        final = layer == DEPTH - 1
        til = _Tiling(*POST_TILING["first" if layer == 0 else "last" if final else "middle"])
        rq, rk, rv, rg, gq, gk, gv, gr, dq, dk, dv, lr = _proj(
            layer, batch, t_all, stream, mods, norm1, w_in_a, w_in_d, tabs)
        ro = _ret(layer, rq, rk, rv, rg, ret_decay_logit)
        go = _gla(layer, gq, gk, gv, lr, gr, wgf, wgb, gla_b_gate, gla_nw)
        do = _attn(layer, dq, dk, dv, diff_lambda, diff_nw)
        stream = _post(layer, batch, t_all, stream, ro, go, do, mods, w_out_b, norm2, w_ffn_in_b, w_ffn_out_b,
                       final_w, til, final)
    return stream
```

```python
import functools
import math

import jax
import jax.numpy as jnp
import numpy as np
from jax import lax
from jax.experimental import pallas as pl
from jax.experimental.pallas import tpu as pltpu

F32 = jnp.float32
BF16 = jnp.bfloat16

D = 1024
DEPTH = 4
CTX = 256
GRID_W = 64
EPS = 1e-6
GN_EPS = 1e-5
ROPE_BASE = 10000.0

NH = 4
RET_DK, RET_DV = 32, 64
GLA_DK, GLA_DV = 32, 64
GLA_RANK = 16
GLA_TAU = 16.0
DIFF_HD, DIFF_DV = 64, 128
D_FF = 2816

O_RQ, O_RK, O_RV, O_RG = 0, 128, 256, 512
O_GQ, O_GK, O_GV, O_GR = 768, 896, 1024, 1280
O_LR = 1536
N_PROJ_A = O_LR + 2 * GLA_RANK

LANE = 128
PROJ_SUBTILES = 3
POST_TILING = {"first": (256, 3, 0), "middle": (288, 2, 0), "last": (256, 2, 1)}
TQ = 256
RET_C = 128
GLA_C = 128
CHUNK_UNROLL = 9
CHUNK_GROUP = 9
VMEM_LIMIT = 56 * 1024 * 1024


def _mm(a, b):
    return jnp.dot(a, b, preferred_element_type=F32)


def _mm_nt(a, b):
    return lax.dot_general(a, b, (((1,), (1,)), ((), ())), preferred_element_type=F32)


def _mm_tn(a, b):
    return lax.dot_general(a, b, (((0,), (0,)), ((), ())), preferred_element_type=F32)


def _split(x):
    hi = x.astype(BF16)
    lo = (x - hi.astype(F32)).astype(BF16)
    return hi, lo


def _hilo_mm(m_bf16, x):
    hi, lo = _split(x)
    n = x.shape[1]
    r = _mm(m_bf16, jnp.concatenate([hi, lo], axis=1))
    return r[:, :n] + r[:, n:]


def _group_mean(x, avg_bf16):
    hi, lo = _split(x)
    m = x.shape[0]
    r = _mm(jnp.concatenate([hi, lo], axis=0), avg_bf16)
    return r[:m] + r[m:]


def _log_sigmoid(z):
    return jnp.minimum(z, 0.0) - jnp.log(1.0 + jnp.exp(-jnp.abs(z)))


def _silu(z):
    return z / (1.0 + jnp.exp(-z))


def _iota(shape, dim):
    return lax.broadcasted_iota(jnp.int32, shape, dim)


def _group_avg_matrix(n, group_shift):
    r = _iota((n, n), 0) >> group_shift
    c = _iota((n, n), 1) >> group_shift
    return jnp.where(r == c, 1.0 / (1 << group_shift), 0.0).astype(BF16)


def _ada_kernel(cond_ref, w_ref, b_ref, o_ref):
    cond = _silu(cond_ref[...])
    w = w_ref[0]
    c_hi, c_lo = _split(cond)
    w_hi, w_lo = _split(w)
    o_ref[0] = _mm(c_hi, w_hi) + _mm(c_hi, w_lo) + _mm(c_lo, w_hi) + b_ref[0]


def _ada(cond, w_ada, b_ada):
    bn = 1536
    n = w_ada.shape[-1]
    return pl.pallas_call(
        _ada_kernel,
        grid=(DEPTH, n // bn),
        in_specs=[
            pl.BlockSpec((16, D), lambda l, j: (0, 0)),
            pl.BlockSpec((1, D, bn), lambda l, j: (l, 0, j)),
            pl.BlockSpec((1, 1, bn), lambda l, j: (l, 0, j)),
        ],
        out_specs=pl.BlockSpec((1, 16, bn), lambda l, j: (l, 0, j)),
        out_shape=jax.ShapeDtypeStruct((DEPTH, 16, n), F32),
        compiler_params=pltpu.CompilerParams(dimension_semantics=("arbitrary", "arbitrary")),
        name="ada",
    )(cond, w_ada, b_ada.reshape(DEPTH, 1, n))


def _mod_specs(layer, j, batch):
    lat = pl.BlockSpec((None, 1, D), lambda b, t: ((layer * 16 + b) * 6 + j, 0, 0))
    ctx = pl.BlockSpec((None, 1, D), lambda b, t: ((layer * 16 + batch) * 6 + j, 0, 0))
    return [lat, ctx]


class _Tiling:
    def __init__(self, sub, n_sub, s0=0):
        self.sub, self.n_sub, self.s0 = sub, n_sub, s0
        self.rows = sub * n_sub
        self.spans = [(j * sub, (j + 1) * sub) for j in range(n_sub)]

    def s(self, j, t=None):
        return (pl.program_id(1) if t is None else t) * self.n_sub + j + self.s0

    def in_specs(self, arr, width):
        specs, ops = [], []
        for j in range(self.n_sub):
            if isinstance(arr, tuple):
                assert CTX % self.sub == 0
                n_ctx = CTX // self.sub
                specs += [
                    pl.BlockSpec((1, self.sub, width), lambda b, t, j=j: (b, jnp.minimum(self.s(j, t), n_ctx - 1), 0)),
                    pl.BlockSpec((1, self.sub, width), lambda b, t, j=j: (b, jnp.maximum(self.s(j, t) - n_ctx, 0), 0))]
                ops += list(arr)
            else:
                specs.append(pl.BlockSpec((1, self.sub, width), lambda b, t, j=j: (b, self.s(j, t), 0)))
                ops.append(arr)
        return specs, ops

    def load(self, refs, j):
        if len(refs) == 2 * self.n_sub:
            return jnp.where(self.s(j) < CTX // self.sub, refs[2 * j][0], refs[2 * j + 1][0])
        return refs[j][0]

    def mod(self, lat_ref, ctx_ref, j):
        if CTX % self.sub == 0:
            return jnp.where(self.s(j) < CTX // self.sub, ctx_ref[...], lat_ref[...])
        assert self.sub > CTX and self.s0 == 0
        if j > 0:
            return lat_ref[...]
        ctx_rows = (_iota((self.sub, 1), 0) < CTX) & (pl.program_id(1) == 0)
        return jnp.where(ctx_rows, ctx_ref[...], lat_ref[...])


def _rope(x, cos, sa, sb):
    outs = []
    for j in range(x.shape[1] // LANE):
        blk = x[:, j * LANE:(j + 1) * LANE]
        fwd = pltpu.roll(blk, LANE - 16, 1)
        bwd = pltpu.roll(blk, 16, 1)
        outs.append(blk * cos + fwd * sa + bwd * sb)
    return outs[0] if len(outs) == 1 else jnp.concatenate(outs, axis=1)


def _proj_kernel(*refs, n_stream, til):
    (sh_l, sh_c, sc_l, sc_c, nw_ref, w_ref, rcos, rsa, rsb, dcos, dsa, dsb,
     rq_o, rk_o, rv_o, rg_o, gq_o, gk_o, gv_o, gr_o, dq_o, dk_o, dv_o, lr_o, wa_ref, wd_ref) = refs[n_stream:]

    @pl.when((pl.program_id(0) == 0) & (pl.program_id(1) == 0))
    def _():
        wa_ref[...] = w_ref[0, :, 0:N_PROJ_A]
        wd_ref[...] = w_ref[0, :, N_PROJ_A:]

    wa, wd = wa_ref[...], wd_ref[...]
    h = []
    for j in range(til.n_sub):
        xj = til.load(refs[:n_stream], j)
        y = xj * lax.rsqrt(jnp.mean(xj * xj, axis=-1, keepdims=True) + EPS) * nw_ref[...]
        h.append((y * (1.0 + til.mod(sc_l, sc_c, j)) + til.mod(sh_l, sh_c, j)).astype(BF16))
    p = [(_mm(hj, wa), _mm(hj, wd)) for hj in h]

    avg = _group_avg_matrix(NH * RET_DV, 6)
    for (r0, r1), (pa, pd) in zip(til.spans, p):
        rc, ra, rb = rcos[r0:r1, :], rsa[r0:r1, :], rsb[r0:r1, :]
        rq_o[0, r0:r1, :] = _rope(pa[:, O_RQ:O_RQ + 128], rc, ra, rb).astype(BF16)
        rk_o[0, r0:r1, :] = (_rope(pa[:, O_RK:O_RK + 128], rc, ra, rb) * (RET_DK ** -0.5)).astype(BF16)
        rv = pa[:, O_RV:O_RV + 256]
        rv_o[0, r0:r1, :] = (rv - _group_mean(rv, avg)).astype(BF16)
        rg_o[0, r0:r1, :] = pa[:, O_RG:O_RG + 256]
        gq_o[0, r0:r1, :] = (pa[:, O_GQ:O_GQ + 128] * (GLA_DK ** -0.5)).astype(BF16)
        gk_o[0, r0:r1, :] = pa[:, O_GK:O_GK + 128].astype(BF16)
        gv_o[0, r0:r1, :] = pa[:, O_GV:O_GV + 256].astype(BF16)
        gr_o[0, r0:r1, :] = pa[:, O_GR:O_GR + 256]
        lr_o[0, r0:r1, :] = pa[:, O_LR:O_LR + 2 * GLA_RANK].astype(BF16)
        dc, da, db = dcos[r0:r1, :], dsa[r0:r1, :], dsb[r0:r1, :]
        dq_o[0, r0:r1, :] = (_rope(pd[:, 0:512], dc, da, db)
                             * (DIFF_HD ** -0.5 * math.log2(math.e))).astype(BF16)
        dk_o[0, r0:r1, :] = _rope(pd[:, 512:1024], dc, da, db).astype(BF16)
        dv_o[0, r0:r1, :] = pd[:, 1024:1536].astype(BF16)


def _proj(layer, batch, t_all, stream, mods, norm1_w, w_in_b, tabs):
    til = _Tiling(CTX, PROJ_SUBTILES)
    tm = til.rows
    tok = lambda n: pl.BlockSpec((1, tm, n), lambda b, t: (b, t, 0))
    tab = pl.BlockSpec((tm, LANE), lambda b, t: (t, 0))
    widths = [128, 128, 256, 256, 128, 128, 256, 256, 512, 512, 512, 2 * GLA_RANK]
    dtypes = [BF16, BF16, BF16, F32, BF16, BF16, BF16, F32, BF16, BF16, BF16, BF16]
    stream_specs, stream_ops = til.in_specs(stream, D)
    n_proj = w_in_b.shape[-1]
    return pl.pallas_call(
        functools.partial(_proj_kernel, n_stream=len(stream_ops), til=til),
        grid=(batch, t_all // tm),
        in_specs=stream_specs + _mod_specs(layer, 0, batch) + _mod_specs(layer, 1, batch) + [
            pl.BlockSpec((None, 1, D), lambda b, t: (layer, 0, 0)),
            pl.BlockSpec((1, D, n_proj), lambda b, t: (layer, 0, 0), pipeline_mode=pl.Buffered(1)),
            tab, tab, tab, tab, tab, tab,
        ],
        out_specs=[tok(n) for n in widths],
        out_shape=[jax.ShapeDtypeStruct((batch, t_all, n), dt) for n, dt in zip(widths, dtypes)],
        scratch_shapes=[pltpu.VMEM((D, N_PROJ_A), BF16), pltpu.VMEM((D, n_proj - N_PROJ_A), BF16)],
        compiler_params=pltpu.CompilerParams(
            dimension_semantics=("arbitrary", "arbitrary"), vmem_limit_bytes=VMEM_LIMIT),
        name="proj",
    )(*stream_ops, mods, mods, mods, mods, norm1_w, w_in_b, *tabs)


def _ret_kernel(logit_ref, q_ref, k_ref, v_ref, g_ref, o_ref, ut, st):
    n_chunks = q_ref.shape[1] // RET_C
    n_ctx = CTX // RET_C
    c = RET_C

    lane_head = _iota((c, LANE), 1) >> 5
    i_f = _iota((c, LANE), 0).astype(F32)

    def lanes_of(d):
        z = jnp.zeros((c, LANE), F32)
        for h in range(NH):
            z = jnp.where(lane_head == h, logit_ref[d, h], z)
        return _log_sigmoid(z)

    lg_f, lg_b = lanes_of(0), lanes_of(1)
    kdec_f = jnp.exp((c - 1.0 - i_f) * lg_f)
    kdec_b = jnp.exp(i_f * lg_b)
    qdec_f = jnp.exp((i_f + 1.0) * lg_f)
    qdec_b = jnp.exp((c - i_f) * lg_b)
    cd_f = jnp.exp(c * lg_f[0:1])
    cd_b = jnp.exp(c * lg_b[0:1])
    kdec = jnp.concatenate([kdec_f, kdec_b], axis=1)
    qdec = jnp.concatenate([qdec_f, qdec_b], axis=1)

    nd = NH * RET_DV
    bd = (_iota((nd, 2 * LANE), 0) >> 6) == ((_iota((nd, 2 * LANE), 1) & (LANE - 1)) >> 5)

    def p0(n, carry):
        rows = pl.ds(pl.multiple_of(n * c, c), c)
        k = k_ref[0, rows, :].astype(F32)
        k2 = (jnp.concatenate([k, k], axis=1) * kdec).astype(BF16)
        ut[n] = jnp.where(bd, _mm_tn(v_ref[0, rows, :], k2), 0.0)
        return carry

    lax.fori_loop(0, n_chunks, p0, 0, unroll=CHUNK_UNROLL)

    s = jnp.zeros((nd, LANE), F32)
    for n in range(n_chunks):
        st[n, :, 0:LANE] = s.astype(BF16)
        s = s * cd_f + ut[n, :, 0:LANE]
    s = jnp.zeros((nd, LANE), F32)
    for n in list(range(n_ctx - 1, -1, -1)) + list(range(n_chunks - 1, n_ctx - 1, -1)):
        st[n, :, LANE:] = s.astype(BF16)
        s = s * cd_b + ut[n, :, LANE:]

    rel = (_iota((c, c), 0) - _iota((c, c), 1)).astype(F32)
    eye = jnp.where(rel == 0.0, 1.0, 0.0)
    dmats = []
    for h in range(NH):
        lf = _log_sigmoid(jnp.full((c, c), logit_ref[0, h], F32))
        lb = _log_sigmoid(jnp.full((c, c), logit_ref[1, h], F32))
        dmats.append(jnp.exp(jnp.where(rel >= 0.0, rel * lf, -rel * lb)) + eye)
    dwide = jnp.concatenate(dmats, axis=1)
    vhead = _iota((c, NH * RET_DV), 1) >> 6
    avg = _group_avg_matrix(NH * RET_DV, 6)

    def p2(i, carry):
        ns = [i * CHUNK_GROUP + j for j in range(CHUNK_GROUP)]
        rows = [pl.ds(pl.multiple_of(n * c, c), c) for n in ns]
        sc, inter = [], []
        for n, r in zip(ns, rows):
            q = q_ref[0, r, :]
            k = k_ref[0, r, :].astype(F32)
            kbd = jnp.concatenate([jnp.where(lane_head == h, k, 0.0).astype(BF16) for h in range(NH)], axis=0)
            sc.append(_mm_nt(q, kbd))
            qf = q.astype(F32)
            inter.append(_mm_nt((jnp.concatenate([qf, qf], axis=1) * qdec).astype(BF16), st[n]))
        o = []
        for r, s_c, o_c in zip(rows, sc, inter):
            v = v_ref[0, r, :].astype(F32)
            vbd = jnp.concatenate([jnp.where(vhead == h, v, 0.0).astype(BF16) for h in range(NH)], axis=0)
            o.append(_mm((s_c * dwide).astype(BF16), vbd) + o_c)
        var = [_mm((o_c * o_c).astype(BF16), avg) for o_c in o]
        for r, o_c, v_c in zip(rows, o, var):
            o_ref[0, r, :] = (_silu(g_ref[0, r, :]) * o_c * lax.rsqrt(v_c + GN_EPS)).astype(BF16)
        return carry

    lax.fori_loop(0, n_chunks // CHUNK_GROUP, p2, 0)


def _ret(layer, q, k, v, g, ret_decay_logit):
    batch, t_all, _ = q.shape
    n_chunks = t_all // RET_C
    tok = lambda n: pl.BlockSpec((1, t_all, n), lambda b, lg: (b, 0, 0))
    return pl.pallas_call(
        _ret_kernel,
        grid_spec=pltpu.PrefetchScalarGridSpec(
            num_scalar_prefetch=1,
            grid=(batch,),
            in_specs=[tok(128), tok(128), tok(256), tok(256)],
            out_specs=tok(256),
            scratch_shapes=[
                pltpu.VMEM((n_chunks, NH * RET_DV, 2 * LANE), F32),
                pltpu.VMEM((n_chunks, NH * RET_DV, 2 * LANE), BF16),
            ],
        ),
        out_shape=jax.ShapeDtypeStruct((batch, t_all, 256), BF16),
        compiler_params=pltpu.CompilerParams(
            dimension_semantics=("arbitrary",), vmem_limit_bytes=VMEM_LIMIT),
        name="ret",
    )(ret_decay_logit[layer], q, k, v, g)


def _gla_kernel(q_ref, k_ref, v_ref, lr_ref, g_ref, wgf_ref, wgb_ref, bg_ref, nw_ref, o_ref,
                cf_s, cb_s, ut, dec, st):
    c = GLA_C
    n_chunks = q_ref.shape[1] // c
    n_ctx = CTX // c
    nd = NH * GLA_DV

    ltri = jnp.where(_iota((c, c), 0) >= _iota((c, c), 1), 1.0, 0.0)
    utri = jnp.where(_iota((c, c), 0) <= _iota((c, c), 1), 1.0, 0.0)
    ltri_b, utri_b = ltri.astype(BF16), utri.astype(BF16)
    bd = (_iota((nd, 2 * LANE), 0) >> 6) == ((_iota((nd, 2 * LANE), 1) & (LANE - 1)) >> 5)

    lr = lr_ref[0]
    cf_s[...] = _log_sigmoid(_mm(lr, wgf_ref[...]) + bg_ref[0:1, :]) * (1.0 / GLA_TAU)
    cb_s[...] = _log_sigmoid(_mm(lr, wgb_ref[...]) + bg_ref[1:2, :]) * (1.0 / GLA_TAU)

    def p0(i, carry):
        ns = [i * CHUNK_GROUP + j for j in range(CHUNK_GROUP)]
        rows = [pl.ds(pl.multiple_of(n * c, c), c) for n in ns]
        cf = [_hilo_mm(ltri_b, cf_s[r, :]) for r in rows]
        cb = [_hilo_mm(utri_b, cb_s[r, :]) for r in rows]
        for n, r, cf_c, cb_c in zip(ns, rows, cf, cb):
            cf_s[r, :] = cf_c
            cb_s[r, :] = cb_c
            cf_end, cb_end = cf_c[c - 1:c, :], cb_c[0:1, :]
            k = k_ref[0, r, :].astype(F32)
            k2 = jnp.concatenate([k * jnp.exp(cf_end - cf_c), k * jnp.exp(cb_end - cb_c)], axis=1).astype(BF16)
            ut[n] = jnp.where(bd, _mm_tn(v_ref[0, r, :], k2), 0.0)
            dec[n] = jnp.broadcast_to(jnp.exp(jnp.concatenate([cf_end, cb_end], axis=1)), (8, 2 * LANE))
        return carry

    lax.fori_loop(0, n_chunks // CHUNK_GROUP, p0, 0)

    def scan_f(n, s):
        st[n, :, 0:LANE] = s.astype(BF16)
        return s * dec[n, 0:1, 0:LANE] + ut[n, :, 0:LANE]

    def scan_b(i, s, hi):
        n = hi - 1 - i
        st[n, :, LANE:] = s.astype(BF16)
        return s * dec[n, 0:1, LANE:] + ut[n, :, LANE:]

    zero = jnp.zeros((nd, LANE), F32)
    lax.fori_loop(0, n_chunks, scan_f, zero)
    s_ctx = lax.fori_loop(0, n_ctx, functools.partial(scan_b, hi=n_ctx), zero)
    lax.fori_loop(0, n_chunks - n_ctx, functools.partial(scan_b, hi=n_chunks), s_ctx)

    lane_head = _iota((c, LANE), 1) >> 5
    vhead = _iota((c, nd), 1) >> 6
    avg = _group_avg_matrix(nd, 6)
    nw = nw_ref[...]
    ltri_w = jnp.concatenate([ltri] * NH, axis=1)
    utri_w = jnp.concatenate([utri] * NH, axis=1)

    def stack_heads(x, head_of_lane):
        return jnp.concatenate([jnp.where(head_of_lane == h, x, 0.0).astype(BF16) for h in range(NH)], axis=0)

    def p2(i, carry):
        ns = [i * CHUNK_GROUP + j for j in range(CHUNK_GROUP)]
        rows = [pl.ds(pl.multiple_of(n * c, c), c) for n in ns]
        s_fb, inter = [], []
        for n, r in zip(ns, rows):
            q = q_ref[0, r, :].astype(F32)
            k = k_ref[0, r, :].astype(F32)
            cf, cb = cf_s[r, :], cb_s[r, :]
            rf, rb = cf[c // 2 - 1:c // 2, :], cb[c // 2:c // 2 + 1, :]
            qf, qb = (q * jnp.exp(cf - rf)).astype(BF16), (q * jnp.exp(cb - rb)).astype(BF16)
            zq = jnp.zeros_like(qf)
            q_fb = jnp.concatenate([jnp.concatenate([qf, zq], axis=1), jnp.concatenate([zq, qb], axis=1)], axis=0)
            k_fb = jnp.concatenate([stack_heads(k * jnp.exp(rf - cf), lane_head),
                                    stack_heads(k * jnp.exp(rb - cb), lane_head)], axis=1)
            s_fb.append(_mm_nt(q_fb, k_fb))
            q2 = jnp.concatenate([q * jnp.exp(cf), q * jnp.exp(cb)], axis=1).astype(BF16)
            inter.append(_mm_nt(q2, st[n]))
        o = []
        for r, s_c, o_c in zip(rows, s_fb, inter):
            a = (s_c[:c] * ltri_w + s_c[c:] * utri_w).astype(BF16)
            o.append(_mm(a, stack_heads(v_ref[0, r, :].astype(F32), vhead)) + o_c)
        ms = [_mm((o_c * o_c).astype(BF16), avg) for o_c in o]
        for r, o_c, ms_c in zip(rows, o, ms):
            y = o_c * lax.rsqrt(ms_c + EPS) * nw
            o_ref[0, r, :] = (_silu(g_ref[0, r, :]) * y).astype(BF16)
        return carry

    lax.fori_loop(0, n_chunks // CHUNK_GROUP, p2, 0)


def _gla(layer, q, k, v, lr, g, wgf, wgb, bg, nw):
    batch, t_all, _ = q.shape
    n_chunks = t_all // GLA_C
    nd = NH * GLA_DV
    tok = lambda n: pl.BlockSpec((1, t_all, n), lambda b: (b, 0, 0))
    par = lambda r, n: pl.BlockSpec((None, r, n), lambda b: (layer, 0, 0))
    return pl.pallas_call(
        _gla_kernel,
        grid=(batch,),
        in_specs=[tok(128), tok(128), tok(256), tok(2 * GLA_RANK), tok(256),
                  par(2 * GLA_RANK, 128), par(2 * GLA_RANK, 128), par(2, 128), par(1, 256)],
        out_specs=tok(256),
        out_shape=jax.ShapeDtypeStruct((batch, t_all, 256), BF16),
        scratch_shapes=[
            pltpu.VMEM((t_all, LANE), F32), pltpu.VMEM((t_all, LANE), F32),
            pltpu.VMEM((n_chunks, nd, 2 * LANE), F32),
            pltpu.VMEM((n_chunks, 8, 2 * LANE), F32),
            pltpu.VMEM((n_chunks, nd, 2 * LANE), BF16),
        ],
        compiler_params=pltpu.CompilerParams(
            dimension_semantics=("arbitrary",), vmem_limit_bytes=VMEM_LIMIT),
        name="gla",
    )(q, k, v, lr, g, wgf, wgb, bg, nw)


def _attn_kernel(q_ref, k_ref, v_ref, lam_ref, nw_ref, o_ref, vx_ref, *, lam_init):
    t_all = k_ref.shape[1]
    vx_ref[:, 0:DIFF_DV] = v_ref[0]
    vx_ref[:, DIFF_DV:] = jnp.ones((t_all, DIFF_DV), BF16)
    lp = lam_ref[...]
    lam = (jnp.exp(jnp.sum(lp[0:1] * lp[1:2], axis=-1, keepdims=True))
           - jnp.exp(jnp.sum(lp[2:3] * lp[3:4], axis=-1, keepdims=True)) + lam_init)
    first = _iota((TQ, LANE), 1) < DIFF_HD

    assert CTX == TQ
    tiles = [(r, CTX if r == 0 else t_all) for r in range(0, t_all, TQ)]
    s = []
    for r, n_keys in tiles:
        q = q_ref[0, r:r + TQ, :]
        for qh in (jnp.where(first, q, jnp.zeros_like(q)), jnp.where(first, jnp.zeros_like(q), q)):
            s.append(_mm_nt(qh, k_ref[0, 0:n_keys, :]).astype(BF16))
    e = [jnp.exp2(s_h - jnp.max(s_h, axis=-1, keepdims=True)) for s_h in s]
    ol = [_mm(e_h, vx_ref[0:e_h.shape[1], :]) for e_h in e]
    for j, (r, _) in enumerate(tiles):
        o1, o2 = (x[:, 0:DIFF_DV] / x[:, DIFF_DV:] for x in ol[2 * j:2 * j + 2])
        o = o1 - lam * o2
        y = o * lax.rsqrt(jnp.mean(o * o, axis=-1, keepdims=True) + EPS) * nw_ref[...]
        o_ref[0, r:r + TQ, :] = (y * (1.0 - lam_init)).astype(BF16)


def _attn(layer, dq, dk, dv, diff_lambda, diff_norm_w):
    batch, t_all, _ = dq.shape
    lam_init = 0.8 - 0.6 * math.exp(-0.3 * layer)
    head = pl.BlockSpec((1, t_all, LANE), lambda b, h: (b, 0, h))
    return pl.pallas_call(
        functools.partial(_attn_kernel, lam_init=lam_init),
        grid=(batch, NH),
        in_specs=[
            head, head, head,
            pl.BlockSpec((None, 4, DIFF_HD), lambda b, h: (layer, 0, 0)),
            pl.BlockSpec((None, 1, DIFF_DV), lambda b, h: (layer, 0, 0)),
        ],
        out_specs=head,
        out_shape=jax.ShapeDtypeStruct((batch, t_all, NH * DIFF_DV), BF16),
        scratch_shapes=[pltpu.VMEM((t_all, 2 * DIFF_DV), BF16)],
        compiler_params=pltpu.CompilerParams(
            dimension_semantics=("arbitrary", "arbitrary"), vmem_limit_bytes=VMEM_LIMIT),
        name="attn",
    )(dq, dk, dv, diff_lambda, diff_norm_w)


def _post_kernel(*refs, n_stream, final, til):
    n = til.n_sub
    x_refs = refs[:n_stream]
    ro_refs, go_refs, do_refs = (refs[n_stream + i * n:n_stream + (i + 1) * n] for i in range(3))
    (g1_l, g1_c, sh_l, sh_c, sc_l, sc_c, g2_l, g2_c,
     wo_ref, nw_ref, wi_ref, wf_ref, fw_ref, o_ref) = refs[n_stream + 3 * n:]

    def rms(v, w_ref):
        return v * lax.rsqrt(jnp.mean(v * v, axis=-1, keepdims=True) + EPS) * w_ref[...]

    subs = range(n)
    att = [_mm(jnp.concatenate([ro_refs[j][0], go_refs[j][0], do_refs[j][0]], axis=-1), wo_ref[0]) for j in subs]
    xm = [til.load(x_refs, j) + til.mod(g1_l, g1_c, j) * att[j] for j in subs]
    h = [(rms(xm[j], nw_ref) * (1.0 + til.mod(sc_l, sc_c, j)) + til.mod(sh_l, sh_c, j)).astype(BF16) for j in subs]
    gu = [_mm(h[j], wi_ref[0]) for j in subs]
    ff = [_mm((_silu(g[:, :D_FF]) * g[:, D_FF:]).astype(BF16), wf_ref[0]) for g in gu]
    for j, (r0, r1) in enumerate(til.spans):
        xo = xm[j] + til.mod(g2_l, g2_c, j) * ff[j]
        o_ref[0, r0:r1, :] = rms(xo, fw_ref) if final else xo


def _post(layer, batch, t_all, stream, ro, go, do, mods, w_out, norm2_w, w_ffn_in, w_ffn_out, final_w, til, final):
    n_rows = t_all - til.s0 * til.sub
    wspec = lambda r, n: pl.BlockSpec((1, r, n), lambda b, t: (layer, 0, 0), pipeline_mode=pl.Buffered(1))
    stream_specs, stream_ops = til.in_specs(stream, D)
    tok_specs, tok_ops = zip(*(til.in_specs(a, a.shape[-1]) for a in (ro, go, do)))
    mod_specs = sum((_mod_specs(layer, j, batch) for j in (2, 3, 4, 5)), [])
    return pl.pallas_call(
        functools.partial(_post_kernel, n_stream=len(stream_ops), final=final, til=til),
        grid=(batch, n_rows // til.rows),
        in_specs=stream_specs + sum(tok_specs, []) + mod_specs + [
            wspec(D, D),
            pl.BlockSpec((None, 1, D), lambda b, t: (layer, 0, 0)),
            wspec(D, 2 * D_FF), wspec(D_FF, D),
            pl.BlockSpec((1, D), lambda b, t: (0, 0)),
        ],
        out_specs=pl.BlockSpec((1, til.rows, D), lambda b, t: (b, t, 0)),
        out_shape=jax.ShapeDtypeStruct((batch, n_rows, D), F32),
        compiler_params=pltpu.CompilerParams(
            dimension_semantics=("arbitrary", "arbitrary"), vmem_limit_bytes=VMEM_LIMIT),
        name="post",
    )(*stream_ops, *sum(tok_ops, []), *([mods] * 8), w_out, norm2_w, w_ffn_in, w_ffn_out, final_w)


def _rope_tables(n_lat):
    lane = np.arange(LANE)
    first = ((lane % 32) < 16)[None, :]

    def pack(ang):
        cos, sin = np.cos(ang), np.sin(ang)
        tabs = (cos, np.where(first, -sin, 0.0), np.where(first, 0.0, sin))
        ident = (np.ones((CTX, LANE)), np.zeros((CTX, LANE)), np.zeros((CTX, LANE)))
        return tuple(jnp.asarray(np.concatenate([i, t], axis=0), dtype=F32) for i, t in zip(ident, tabs))

    idx = np.arange(n_lat, dtype=np.float64)
    ret_freq = 1.0 / (ROPE_BASE ** np.linspace(0.0, 1.0, RET_DK // 2))
    ret = pack((idx[:, None] * ret_freq[None, :])[:, lane % 16])
    ax_freq = 1.0 / (ROPE_BASE ** (np.arange(DIFF_HD // 4) / (DIFF_HD // 4)))
    row_ang = np.floor(idx / GRID_W)[:, None] * ax_freq[None, :]
    col_ang = (idx % GRID_W)[:, None] * ax_freq[None, :]
    is_row = ((lane % 64) < 32)[None, :]
    diff = pack(np.where(is_row, row_ang[:, lane % 16], col_ang[:, lane % 16]))
    return ret + diff


def kernel(x, c, ctx, c_ctx, w_ada, b_ada, norm1_w, w_in, ret_decay_logit, gla_w_gate, gla_b_gate,
           gla_norm_w, diff_lambda, diff_norm_w, w_out, norm2_w, w_ffn_in, w_ffn_out, final_norm_w):
    batch, n_lat, _ = x.shape
    assert ctx.shape[1] == CTX and batch < 16

    w_in_b = w_in.astype(BF16)
    w_out_b = w_out.astype(BF16)
    w_ffn_in_b = w_ffn_in.astype(BF16)
    w_ffn_out_b = w_ffn_out.astype(BF16)
    zpad = jnp.zeros((DEPTH, GLA_RANK, NH * GLA_DK), F32)
    wgf = jnp.concatenate([gla_w_gate[:, 0], zpad], axis=1).astype(BF16)
    wgb = jnp.concatenate([zpad, gla_w_gate[:, 1]], axis=1).astype(BF16)
    gla_nw = jnp.tile(gla_norm_w, (1, NH)).reshape(DEPTH, 1, NH * GLA_DV)
    norm1 = norm1_w.reshape(DEPTH, 1, D)
    norm2 = norm2_w.reshape(DEPTH, 1, D)
    diff_nw = diff_norm_w.reshape(DEPTH, 1, DIFF_DV)
    final_w = final_norm_w.reshape(1, D)
    tabs = _rope_tables(n_lat)

    cond = jnp.concatenate([c, c_ctx[None, :], jnp.zeros((16 - batch - 1, D), F32)], axis=0)
    mods = _ada(cond, w_ada, b_ada).reshape(DEPTH * 16 * 6, 1, D)

    t_all = CTX + n_lat
    stream = (ctx, x)
    for layer in range(DEPTH):
        final = layer == DEPTH - 1
        til = _Tiling(*POST_TILING["first" if layer == 0 else "last" if final else "middle"])
        rq, rk, rv, rg, gq, gk, gv, gr, dq, dk, dv, lr = _proj(
            layer, batch, t_all, stream, mods, norm1, w_in_b, tabs)
        ro = _ret(layer, rq, rk, rv, rg, ret_decay_logit)
        go = _gla(layer, gq, gk, gv, lr, gr, wgf, wgb, gla_b_gate, gla_nw)
        do = _attn(layer, dq, dk, dv, diff_lambda, diff_nw)
        stream = _post(layer, batch, t_all, stream, ro, go, do, mods, w_out_b, norm2, w_ffn_in_b, w_ffn_out_b,
                       final_w, til, final)
    return stream
```

```python
import functools
import math

import jax
import jax.numpy as jnp
import numpy as np
from jax import lax
from jax.experimental import pallas as pl
from jax.experimental.pallas import tpu as pltpu

F32 = jnp.float32
BF16 = jnp.bfloat16

D = 1024
DEPTH = 4
CTX = 256
GRID_W = 64
EPS = 1e-6
GN_EPS = 1e-5
ROPE_BASE = 10000.0

NH = 4
RET_DK, RET_DV = 32, 64
GLA_DK, GLA_DV = 32, 64
GLA_RANK = 16
GLA_TAU = 16.0
DIFF_HD, DIFF_DV = 64, 128
D_FF = 2816

O_RQ, O_RK, O_RV, O_RG = 0, 128, 256, 512
O_GQ, O_GK, O_GV, O_GR = 768, 896, 1024, 1280
O_LR = 1536
N_PROJ_A = O_LR + 2 * GLA_RANK

LANE = 128
PROJ_SUBTILES = 3
POST_TILING = {"first": (256, 3, 0), "middle": (288, 2, 0), "last": (256, 2, 1)}
TQ = 256
RET_C = 128
GLA_C = 128
CHUNK_GROUP = 9
VMEM_LIMIT = 56 * 1024 * 1024


def _mm(a, b):
    return jnp.dot(a, b, preferred_element_type=F32)


def _mm_nt(a, b):
    return lax.dot_general(a, b, (((1,), (1,)), ((), ())), preferred_element_type=F32)


def _mm_tn(a, b):
    return lax.dot_general(a, b, (((0,), (0,)), ((), ())), preferred_element_type=F32)


def _split(x):
    hi = x.astype(BF16)
    lo = (x - hi.astype(F32)).astype(BF16)
    return hi, lo


def _hilo_mm(m_bf16, x):
    hi, lo = _split(x)
    n = x.shape[1]
    r = _mm(m_bf16, jnp.concatenate([hi, lo], axis=1))
    return r[:, :n] + r[:, n:]


def _group_mean(x, avg_bf16):
    hi, lo = _split(x)
    m = x.shape[0]
    r = _mm(jnp.concatenate([hi, lo], axis=0), avg_bf16)
    return r[:m] + r[m:]


def _log_sigmoid(z):
    return jnp.minimum(z, 0.0) - jnp.log(1.0 + jnp.exp(-jnp.abs(z)))


def _silu(z):
    return z / (1.0 + jnp.exp(-z))


def _iota(shape, dim):
    return lax.broadcasted_iota(jnp.int32, shape, dim)


def _head_masks(rows, lanes, head_shift):
    head = _iota((rows, lanes), 1) >> head_shift
    return [head == h for h in range(NH)]


def _stack_heads(x, masks):
    x = x.astype(F32)
    return jnp.concatenate([jnp.where(m, x, 0.0).astype(BF16) for m in masks], axis=0)


def _group_avg_matrix(n, group_shift):
    r = _iota((n, n), 0) >> group_shift
    c = _iota((n, n), 1) >> group_shift
    return jnp.where(r == c, 1.0 / (1 << group_shift), 0.0).astype(BF16)


def _ada_kernel(cond_ref, w_ref, b_ref, o_ref):
    cond = _silu(cond_ref[...])
    w = w_ref[0]
    c_hi, c_lo = _split(cond)
    w_hi, w_lo = _split(w)
    hh = _mm(jnp.concatenate([c_hi, c_lo], axis=0), w_hi)
    o_ref[0] = hh[:16] + hh[16:] + _mm(c_hi, w_lo) + b_ref[0]


def _ada(cond, w_ada, b_ada):
    bn = 1536
    n = w_ada.shape[-1]
    return pl.pallas_call(
        _ada_kernel,
        grid=(DEPTH, n // bn),
        in_specs=[
            pl.BlockSpec((16, D), lambda l, j: (0, 0)),
            pl.BlockSpec((1, D, bn), lambda l, j: (l, 0, j)),
            pl.BlockSpec((1, 1, bn), lambda l, j: (l, 0, j)),
        ],
        out_specs=pl.BlockSpec((1, 16, bn), lambda l, j: (l, 0, j)),
        out_shape=jax.ShapeDtypeStruct((DEPTH, 16, n), F32),
        compiler_params=pltpu.CompilerParams(dimension_semantics=("arbitrary", "arbitrary")),
        name="ada",
    )(cond, w_ada, b_ada.reshape(DEPTH, 1, n))


def _mod_specs(layer, j, batch):
    lat = pl.BlockSpec((None, 1, D), lambda b, t: ((layer * 16 + b) * 6 + j, 0, 0))
    ctx = pl.BlockSpec((None, 1, D), lambda b, t: ((layer * 16 + batch) * 6 + j, 0, 0))
    return [lat, ctx]


class _Tiling:
    def __init__(self, sub, n_sub, s0=0):
        self.sub, self.n_sub, self.s0 = sub, n_sub, s0
        self.rows = sub * n_sub
        self.spans = [(j * sub, (j + 1) * sub) for j in range(n_sub)]

    def s(self, j, t=None):
        return (pl.program_id(1) if t is None else t) * self.n_sub + j + self.s0

    def in_specs(self, arr, width):
        specs, ops = [], []
        for j in range(self.n_sub):
            if isinstance(arr, tuple):
                assert CTX % self.sub == 0
                n_ctx = CTX // self.sub
                specs += [
                    pl.BlockSpec((1, self.sub, width), lambda b, t, j=j: (b, jnp.minimum(self.s(j, t), n_ctx - 1), 0)),
                    pl.BlockSpec((1, self.sub, width), lambda b, t, j=j: (b, jnp.maximum(self.s(j, t) - n_ctx, 0), 0))]
                ops += list(arr)
            else:
                specs.append(pl.BlockSpec((1, self.sub, width), lambda b, t, j=j: (b, self.s(j, t), 0)))
                ops.append(arr)
        return specs, ops

    def load(self, refs, j):
        if len(refs) == 2 * self.n_sub:
            return jnp.where(self.s(j) < CTX // self.sub, refs[2 * j][0], refs[2 * j + 1][0])
        return refs[j][0]

    def mod(self, lat_ref, ctx_ref, j):
        if CTX % self.sub == 0:
            return jnp.where(self.s(j) < CTX // self.sub, ctx_ref[...], lat_ref[...])
        assert self.sub > CTX and self.s0 == 0
        if j > 0:
            return lat_ref[...]
        ctx_rows = (_iota((self.sub, 1), 0) < CTX) & (pl.program_id(1) == 0)
        return jnp.where(ctx_rows, ctx_ref[...], lat_ref[...])


def _rope(x, cos, sa, sb):
    outs = []
    for j in range(x.shape[1] // LANE):
        blk = x[:, j * LANE:(j + 1) * LANE]
        fwd = pltpu.roll(blk, LANE - 16, 1)
        bwd = pltpu.roll(blk, 16, 1)
        outs.append(blk * cos + fwd * sa + bwd * sb)
    return outs[0] if len(outs) == 1 else jnp.concatenate(outs, axis=1)


def _proj_kernel(*refs, n_stream, til):
    (sh_l, sh_c, sc_l, sc_c, nw_ref, w_ref, rcos, rsa, rsb, dcos, dsa, dsb,
     rq_o, rk_o, rv_o, rg_o, gq_o, gk_o, gv_o, gr_o, dq_o, dk_o, dv_o, lr_o, wa_ref, wd_ref) = refs[n_stream:]

    @pl.when((pl.program_id(0) == 0) & (pl.program_id(1) == 0))
    def _():
        wa_ref[...] = w_ref[0, :, 0:N_PROJ_A]
        wd_ref[...] = w_ref[0, :, N_PROJ_A:]

    wa, wd = wa_ref[...], wd_ref[...]
    h = []
    for j in range(til.n_sub):
        xj = til.load(refs[:n_stream], j)
        y = xj * lax.rsqrt(jnp.mean(xj * xj, axis=-1, keepdims=True) + EPS) * nw_ref[...]
        h.append((y * (1.0 + til.mod(sc_l, sc_c, j)) + til.mod(sh_l, sh_c, j)).astype(BF16))
    p = [(_mm(hj, wa), _mm(hj, wd)) for hj in h]

    avg = _group_avg_matrix(NH * RET_DV, 6)
    for (r0, r1), (pa, pd) in zip(til.spans, p):
        rc, ra, rb = rcos[r0:r1, :], rsa[r0:r1, :], rsb[r0:r1, :]
        rq_o[0, r0:r1, :] = _rope(pa[:, O_RQ:O_RQ + 128], rc, ra, rb).astype(BF16)
        rk_o[0, r0:r1, :] = (_rope(pa[:, O_RK:O_RK + 128], rc, ra, rb) * (RET_DK ** -0.5)).astype(BF16)
        rv = pa[:, O_RV:O_RV + 256]
        rv_o[0, r0:r1, :] = (rv - _group_mean(rv, avg)).astype(BF16)
        rg_o[0, r0:r1, :] = pa[:, O_RG:O_RG + 256]
        gq_o[0, r0:r1, :] = (pa[:, O_GQ:O_GQ + 128] * (GLA_DK ** -0.5)).astype(BF16)
        gk_o[0, r0:r1, :] = pa[:, O_GK:O_GK + 128].astype(BF16)
        gv_o[0, r0:r1, :] = pa[:, O_GV:O_GV + 256].astype(BF16)
        gr_o[0, r0:r1, :] = pa[:, O_GR:O_GR + 256]
        lr_o[0, r0:r1, :] = pa[:, O_LR:O_LR + 2 * GLA_RANK].astype(BF16)
        dc, da, db = dcos[r0:r1, :], dsa[r0:r1, :], dsb[r0:r1, :]
        dq_o[0, r0:r1, :] = (_rope(pd[:, 0:512], dc, da, db)
                             * (DIFF_HD ** -0.5 * math.log2(math.e))).astype(BF16)
        dk_o[0, r0:r1, :] = _rope(pd[:, 512:1024], dc, da, db).astype(BF16)
        dv_o[0, r0:r1, :] = pd[:, 1024:1536].astype(BF16)


def _proj(layer, batch, t_all, stream, mods, norm1_w, w_in_b, tabs):
    til = _Tiling(CTX, PROJ_SUBTILES)
    tm = til.rows
    tok = lambda n: pl.BlockSpec((1, tm, n), lambda b, t: (b, t, 0))
    tab = pl.BlockSpec((tm, LANE), lambda b, t: (t, 0))
    widths = [128, 128, 256, 256, 128, 128, 256, 256, 512, 512, 512, 2 * GLA_RANK]
    dtypes = [BF16, BF16, BF16, F32, BF16, BF16, BF16, F32, BF16, BF16, BF16, BF16]
    stream_specs, stream_ops = til.in_specs(stream, D)
    n_proj = w_in_b.shape[-1]
    return pl.pallas_call(
        functools.partial(_proj_kernel, n_stream=len(stream_ops), til=til),
        grid=(batch, t_all // tm),
        in_specs=stream_specs + _mod_specs(layer, 0, batch) + _mod_specs(layer, 1, batch) + [
            pl.BlockSpec((None, 1, D), lambda b, t: (layer, 0, 0)),
            pl.BlockSpec((1, D, n_proj), lambda b, t: (layer, 0, 0), pipeline_mode=pl.Buffered(1)),
            tab, tab, tab, tab, tab, tab,
        ],
        out_specs=[tok(n) for n in widths],
        out_shape=[jax.ShapeDtypeStruct((batch, t_all, n), dt) for n, dt in zip(widths, dtypes)],
        scratch_shapes=[pltpu.VMEM((D, N_PROJ_A), BF16), pltpu.VMEM((D, n_proj - N_PROJ_A), BF16)],
        compiler_params=pltpu.CompilerParams(
            dimension_semantics=("arbitrary", "arbitrary"), vmem_limit_bytes=VMEM_LIMIT),
        name="proj",
    )(*stream_ops, mods, mods, mods, mods, norm1_w, w_in_b, *tabs)


def _ret_phases(logit_ref, q_ref, k_ref, v_ref, g_ref, o_ref, ut, st):
    n_chunks = q_ref.shape[1] // RET_C
    n_ctx = CTX // RET_C
    c = RET_C

    lane_head = _iota((c, LANE), 1) >> 5
    i_f = _iota((c, LANE), 0).astype(F32)

    def lanes_of(d):
        z = jnp.zeros((c, LANE), F32)
        for h in range(NH):
            z = jnp.where(lane_head == h, logit_ref[d, h], z)
        return _log_sigmoid(z)

    lg_f, lg_b = lanes_of(0), lanes_of(1)
    kdec_f = jnp.exp((c - 1.0 - i_f) * lg_f)
    kdec_b = jnp.exp(i_f * lg_b)
    qdec_f = jnp.exp((i_f + 1.0) * lg_f)
    qdec_b = jnp.exp((c - i_f) * lg_b)
    cd_f = jnp.exp(c * lg_f[0:1])
    cd_b = jnp.exp(c * lg_b[0:1])
    kdec = jnp.concatenate([kdec_f, kdec_b], axis=1)
    qdec = jnp.concatenate([qdec_f, qdec_b], axis=1)

    nd = NH * RET_DV
    bd = (_iota((nd, 2 * LANE), 0) >> 6) == ((_iota((nd, 2 * LANE), 1) & (LANE - 1)) >> 5)

    def p0(i):
        for j in range(CHUNK_GROUP):
            n = i * CHUNK_GROUP + j
            rows = pl.ds(pl.multiple_of(n * c, c), c)
            k = k_ref[0, rows, :].astype(F32)
            k2 = (jnp.concatenate([k, k], axis=1) * kdec).astype(BF16)
            ut[n] = jnp.where(bd, _mm_tn(v_ref[0, rows, :], k2), 0.0)
        yield

    def scan():
        s = jnp.zeros((nd, LANE), F32)
        for n in range(n_chunks):
            st[n, :, 0:LANE] = s.astype(BF16)
            s = s * cd_f + ut[n, :, 0:LANE]
        s = jnp.zeros((nd, LANE), F32)
        for n in list(range(n_ctx - 1, -1, -1)) + list(range(n_chunks - 1, n_ctx - 1, -1)):
            st[n, :, LANE:] = s.astype(BF16)
            s = s * cd_b + ut[n, :, LANE:]

    rel = (_iota((c, c), 0) - _iota((c, c), 1)).astype(F32)
    eye = jnp.where(rel == 0.0, 1.0, 0.0)
    dmats = []
    for h in range(NH):
        lf = _log_sigmoid(jnp.full((c, c), logit_ref[0, h], F32))
        lb = _log_sigmoid(jnp.full((c, c), logit_ref[1, h], F32))
        dmats.append(jnp.exp(jnp.where(rel >= 0.0, rel * lf, -rel * lb)) + eye)
    dwide = jnp.concatenate(dmats, axis=1)
    kmasks, vmasks = _head_masks(c, LANE, 5), _head_masks(c, NH * RET_DV, 6)
    avg = _group_avg_matrix(NH * RET_DV, 6)

    def p2(i):
        ns = [i * CHUNK_GROUP + j for j in range(CHUNK_GROUP)]
        rows = [pl.ds(pl.multiple_of(n * c, c), c) for n in ns]
        sc, inter = [], []
        for n, r in zip(ns, rows):
            q = q_ref[0, r, :]
            sc.append(_mm_nt(q, _stack_heads(k_ref[0, r, :], kmasks)))
            qf = q.astype(F32)
            inter.append(_mm_nt((jnp.concatenate([qf, qf], axis=1) * qdec).astype(BF16), st[n]))
        yield
        o = []
        for r, s_c, o_c in zip(rows, sc, inter):
            o.append(_mm((s_c * dwide).astype(BF16), _stack_heads(v_ref[0, r, :], vmasks)) + o_c)
        yield
        var = [_mm((o_c * o_c).astype(BF16), avg) for o_c in o]
        yield
        for r, o_c, v_c in zip(rows, o, var):
            o_ref[0, r, :] = (_silu(g_ref[0, r, :]) * o_c * lax.rsqrt(v_c + GN_EPS)).astype(BF16)
        yield

    return p0, scan, p2


def _gla_phases(q_ref, k_ref, v_ref, lr_ref, g_ref, wgf_ref, wgb_ref, bg_ref, nw_ref, o_ref,
                cf_s, cb_s, ut, dec, st):
    c = GLA_C
    n_chunks = q_ref.shape[1] // c
    n_ctx = CTX // c
    nd = NH * GLA_DV

    ltri = jnp.where(_iota((c, c), 0) >= _iota((c, c), 1), 1.0, 0.0)
    utri = jnp.where(_iota((c, c), 0) <= _iota((c, c), 1), 1.0, 0.0)
    ltri_b, utri_b = ltri.astype(BF16), utri.astype(BF16)
    bd = (_iota((nd, 2 * LANE), 0) >> 6) == ((_iota((nd, 2 * LANE), 1) & (LANE - 1)) >> 5)

    lr = lr_ref[0]
    cf_s[...] = _log_sigmoid(_mm(lr, wgf_ref[...]) + bg_ref[0:1, :]) * (1.0 / GLA_TAU)
    cb_s[...] = _log_sigmoid(_mm(lr, wgb_ref[...]) + bg_ref[1:2, :]) * (1.0 / GLA_TAU)

    def p0(i):
        ns = [i * CHUNK_GROUP + j for j in range(CHUNK_GROUP)]
        rows = [pl.ds(pl.multiple_of(n * c, c), c) for n in ns]
        cf = [_hilo_mm(ltri_b, cf_s[r, :]) for r in rows]
        cb = [_hilo_mm(utri_b, cb_s[r, :]) for r in rows]
        yield
        for n, r, cf_c, cb_c in zip(ns, rows, cf, cb):
            cf_s[r, :] = cf_c
            cb_s[r, :] = cb_c
            cf_end, cb_end = cf_c[c - 1:c, :], cb_c[0:1, :]
            k = k_ref[0, r, :].astype(F32)
            k2 = jnp.concatenate([k * jnp.exp(cf_end - cf_c), k * jnp.exp(cb_end - cb_c)], axis=1).astype(BF16)
            ut[n] = jnp.where(bd, _mm_tn(v_ref[0, r, :], k2), 0.0)
            dec[n] = jnp.broadcast_to(jnp.exp(jnp.concatenate([cf_end, cb_end], axis=1)), (8, 2 * LANE))
        yield

    def scan():
        def scan_f(n, s):
            st[n, :, 0:LANE] = s.astype(BF16)
            return s * dec[n, 0:1, 0:LANE] + ut[n, :, 0:LANE]

        def scan_b(i, s, hi):
            n = hi - 1 - i
            st[n, :, LANE:] = s.astype(BF16)
            return s * dec[n, 0:1, LANE:] + ut[n, :, LANE:]

        zero = jnp.zeros((nd, LANE), F32)
        lax.fori_loop(0, n_chunks, scan_f, zero)
        s_ctx = lax.fori_loop(0, n_ctx, functools.partial(scan_b, hi=n_ctx), zero)
        lax.fori_loop(0, n_chunks - n_ctx, functools.partial(scan_b, hi=n_chunks), s_ctx)

    avg = _group_avg_matrix(nd, 6)
    nw = nw_ref[...]
    ltri_w = jnp.concatenate([ltri] * NH, axis=1)
    utri_w = jnp.concatenate([utri] * NH, axis=1)

    kmasks, vmasks = _head_masks(c, LANE, 5), _head_masks(c, nd, 6)

    def p2(i):
        ns = [i * CHUNK_GROUP + j for j in range(CHUNK_GROUP)]
        rows = [pl.ds(pl.multiple_of(n * c, c), c) for n in ns]
        s_fb, inter = [], []
        for n, r in zip(ns, rows):
            q = q_ref[0, r, :].astype(F32)
            k = k_ref[0, r, :].astype(F32)
            cf, cb = cf_s[r, :], cb_s[r, :]
            rf, rb = cf[c // 2 - 1:c // 2, :], cb[c // 2:c // 2 + 1, :]
            qf, qb = (q * jnp.exp(cf - rf)).astype(BF16), (q * jnp.exp(cb - rb)).astype(BF16)
            zq = jnp.zeros_like(qf)
            q_fb = jnp.concatenate([jnp.concatenate([qf, zq], axis=1), jnp.concatenate([zq, qb], axis=1)], axis=0)
            k_fb = jnp.concatenate([_stack_heads(k * jnp.exp(rf - cf), kmasks),
                                    _stack_heads(k * jnp.exp(rb - cb), kmasks)], axis=1)
            s_fb.append(_mm_nt(q_fb, k_fb))
            q2 = jnp.concatenate([q * jnp.exp(cf), q * jnp.exp(cb)], axis=1).astype(BF16)
            inter.append(_mm_nt(q2, st[n]))
        yield
        o = []
        for r, s_c, o_c in zip(rows, s_fb, inter):
            a = (s_c[:c] * ltri_w + s_c[c:] * utri_w).astype(BF16)
            o.append(_mm(a, _stack_heads(v_ref[0, r, :], vmasks)) + o_c)
        yield
        ms = [_mm((o_c * o_c).astype(BF16), avg) for o_c in o]
        yield
        for r, o_c, ms_c in zip(rows, o, ms):
            y = o_c * lax.rsqrt(ms_c + EPS) * nw
            o_ref[0, r, :] = (_silu(g_ref[0, r, :]) * y).astype(BF16)
        yield

    return p0, scan, p2


def _mix_kernel(logit_ref, rq, rk, rv, rg, gq, gk, gv, lr, gg, wgf, wgb, bg, nw, ro, go,
                r_ut, r_st, cf_s, cb_s, g_ut, g_dec, g_st):
    mixers = [_ret_phases(logit_ref, rq, rk, rv, rg, ro, r_ut, r_st),
              _gla_phases(gq, gk, gv, lr, gg, wgf, wgb, bg, nw, go, cf_s, cb_s, g_ut, g_dec, g_st)]
    n_groups = rq.shape[1] // (RET_C * CHUNK_GROUP)

    def lockstep(phase, i):
        done = object()
        gens = [m[phase](i) for m in mixers]
        while gens:
            gens = [g for g in gens if next(g, done) is not done]

    def p0(i, carry):
        lockstep(0, i)
        return carry

    def p2(i, carry):
        lockstep(2, i)
        return carry

    lax.fori_loop(0, n_groups, p0, 0)
    for m in mixers:
        m[1]()
    lax.fori_loop(0, n_groups, p2, 0)


def _mix(layer, rq, rk, rv, rg, gq, gk, gv, lr, gg, ret_decay_logit, wgf, wgb, bg, nw):
    batch, t_all, _ = rq.shape
    assert RET_C == GLA_C and t_all % (RET_C * CHUNK_GROUP) == 0
    n_chunks = t_all // RET_C
    nd = NH * GLA_DV
    tok = lambda n: pl.BlockSpec((1, t_all, n), lambda b, lg: (b, 0, 0))
    par = lambda r, n: pl.BlockSpec((None, r, n), lambda b, lg: (layer, 0, 0))
    state = lambda dt: pltpu.VMEM((n_chunks, nd, 2 * LANE), dt)
    return pl.pallas_call(
        _mix_kernel,
        grid_spec=pltpu.PrefetchScalarGridSpec(
            num_scalar_prefetch=1,
            grid=(batch,),
            in_specs=[tok(128), tok(128), tok(256), tok(256),
                      tok(128), tok(128), tok(256), tok(2 * GLA_RANK), tok(256),
                      par(2 * GLA_RANK, 128), par(2 * GLA_RANK, 128), par(2, 128), par(1, 256)],
            out_specs=[tok(256), tok(256)],
            scratch_shapes=[
                state(F32), state(BF16),
                pltpu.VMEM((t_all, LANE), F32), pltpu.VMEM((t_all, LANE), F32),
                state(F32), pltpu.VMEM((n_chunks, 8, 2 * LANE), F32), state(BF16),
            ],
        ),
        out_shape=[jax.ShapeDtypeStruct((batch, t_all, 256), BF16)] * 2,
        compiler_params=pltpu.CompilerParams(
            dimension_semantics=("arbitrary",), vmem_limit_bytes=VMEM_LIMIT),
        name="mix",
    )(ret_decay_logit[layer], rq, rk, rv, rg, gq, gk, gv, lr, gg, wgf, wgb, bg, nw)


def _attn_kernel(q_ref, k_ref, v_ref, lam_ref, nw_ref, o_ref, vx_ref, *, lam_init):
    t_all = k_ref.shape[1]
    vx_ref[:, 0:DIFF_DV] = v_ref[0]
    vx_ref[:, DIFF_DV:] = jnp.ones((t_all, DIFF_DV), BF16)
    lp = lam_ref[...]
    lam = (jnp.exp(jnp.sum(lp[0:1] * lp[1:2], axis=-1, keepdims=True))
           - jnp.exp(jnp.sum(lp[2:3] * lp[3:4], axis=-1, keepdims=True)) + lam_init)
    first = _iota((TQ, LANE), 1) < DIFF_HD

    assert CTX == TQ
    tiles = [(r, CTX if r == 0 else t_all) for r in range(0, t_all, TQ)]
    s = []
    for r, n_keys in tiles:
        q = q_ref[0, r:r + TQ, :]
        for qh in (jnp.where(first, q, jnp.zeros_like(q)), jnp.where(first, jnp.zeros_like(q), q)):
            s.append(_mm_nt(qh, k_ref[0, 0:n_keys, :]).astype(BF16))
    e = [jnp.exp2(s_h - jnp.max(s_h, axis=-1, keepdims=True)) for s_h in s]
    ol = [_mm(e_h, vx_ref[0:e_h.shape[1], :]) for e_h in e]
    for j, (r, _) in enumerate(tiles):
        o1, o2 = (x[:, 0:DIFF_DV] / x[:, DIFF_DV:] for x in ol[2 * j:2 * j + 2])
        o = o1 - lam * o2
        y = o * lax.rsqrt(jnp.mean(o * o, axis=-1, keepdims=True) + EPS) * nw_ref[...]
        o_ref[0, r:r + TQ, :] = (y * (1.0 - lam_init)).astype(BF16)


def _attn(layer, dq, dk, dv, diff_lambda, diff_norm_w):
    batch, t_all, _ = dq.shape
    lam_init = 0.8 - 0.6 * math.exp(-0.3 * layer)
    head = pl.BlockSpec((1, t_all, LANE), lambda b, h: (b, 0, h))
    return pl.pallas_call(
        functools.partial(_attn_kernel, lam_init=lam_init),
        grid=(batch, NH),
        in_specs=[
            head, head, head,
            pl.BlockSpec((None, 4, DIFF_HD), lambda b, h: (layer, 0, 0)),
            pl.BlockSpec((None, 1, DIFF_DV), lambda b, h: (layer, 0, 0)),
        ],
        out_specs=head,
        out_shape=jax.ShapeDtypeStruct((batch, t_all, NH * DIFF_DV), BF16),
        scratch_shapes=[pltpu.VMEM((t_all, 2 * DIFF_DV), BF16)],
        compiler_params=pltpu.CompilerParams(
            dimension_semantics=("arbitrary", "arbitrary"), vmem_limit_bytes=VMEM_LIMIT),
        name="attn",
    )(dq, dk, dv, diff_lambda, diff_norm_w)


def _post_kernel(*refs, n_stream, final, til):
    n = til.n_sub
    x_refs = refs[:n_stream]
    ro_refs, go_refs, do_refs = (refs[n_stream + i * n:n_stream + (i + 1) * n] for i in range(3))
    (g1_l, g1_c, sh_l, sh_c, sc_l, sc_c, g2_l, g2_c,
     wo_ref, nw_ref, wi_ref, wf_ref, fw_ref, o_ref) = refs[n_stream + 3 * n:]

    def rms(v, w_ref):
        return v * lax.rsqrt(jnp.mean(v * v, axis=-1, keepdims=True) + EPS) * w_ref[...]

    subs = range(n)
    att = [_mm(jnp.concatenate([ro_refs[j][0], go_refs[j][0], do_refs[j][0]], axis=-1), wo_ref[0]) for j in subs]
    xm = [til.load(x_refs, j) + til.mod(g1_l, g1_c, j) * att[j] for j in subs]
    h = [(rms(xm[j], nw_ref) * (1.0 + til.mod(sc_l, sc_c, j)) + til.mod(sh_l, sh_c, j)).astype(BF16) for j in subs]
    gu = [_mm(h[j], wi_ref[0]) for j in subs]
    ff = [_mm((_silu(g[:, :D_FF]) * g[:, D_FF:]).astype(BF16), wf_ref[0]) for g in gu]
    for j, (r0, r1) in enumerate(til.spans):
        xo = xm[j] + til.mod(g2_l, g2_c, j) * ff[j]
        o_ref[0, r0:r1, :] = rms(xo, fw_ref) if final else xo


def _post(layer, batch, t_all, stream, ro, go, do, mods, w_out, norm2_w, w_ffn_in, w_ffn_out, final_w, til, final):
    n_rows = t_all - til.s0 * til.sub
    wspec = lambda r, n: pl.BlockSpec((1, r, n), lambda b, t: (layer, 0, 0), pipeline_mode=pl.Buffered(1))
    stream_specs, stream_ops = til.in_specs(stream, D)
    tok_specs, tok_ops = zip(*(til.in_specs(a, a.shape[-1]) for a in (ro, go, do)))
    mod_specs = sum((_mod_specs(layer, j, batch) for j in (2, 3, 4, 5)), [])
    return pl.pallas_call(
        functools.partial(_post_kernel, n_stream=len(stream_ops), final=final, til=til),
        grid=(batch, n_rows // til.rows),
        in_specs=stream_specs + sum(tok_specs, []) + mod_specs + [
            wspec(D, D),
            pl.BlockSpec((None, 1, D), lambda b, t: (layer, 0, 0)),
            wspec(D, 2 * D_FF), wspec(D_FF, D),
            pl.BlockSpec((1, D), lambda b, t: (0, 0)),
        ],
        out_specs=pl.BlockSpec((1, til.rows, D), lambda b, t: (b, t, 0)),
        out_shape=jax.ShapeDtypeStruct((batch, n_rows, D), F32),
        compiler_params=pltpu.CompilerParams(
            dimension_semantics=("arbitrary", "arbitrary"), vmem_limit_bytes=VMEM_LIMIT),
        name="post",
    )(*stream_ops, *sum(tok_ops, []), *([mods] * 8), w_out, norm2_w, w_ffn_in, w_ffn_out, final_w)


def _rope_tables(n_lat):
    lane = np.arange(LANE)
    first = ((lane % 32) < 16)[None, :]

    def pack(ang):
        cos, sin = np.cos(ang), np.sin(ang)
        tabs = (cos, np.where(first, -sin, 0.0), np.where(first, 0.0, sin))
        ident = (np.ones((CTX, LANE)), np.zeros((CTX, LANE)), np.zeros((CTX, LANE)))
        return tuple(jnp.asarray(np.concatenate([i, t], axis=0), dtype=F32) for i, t in zip(ident, tabs))

    idx = np.arange(n_lat, dtype=np.float64)
    ret_freq = 1.0 / (ROPE_BASE ** np.linspace(0.0, 1.0, RET_DK // 2))
    ret = pack((idx[:, None] * ret_freq[None, :])[:, lane % 16])
    ax_freq = 1.0 / (ROPE_BASE ** (np.arange(DIFF_HD // 4) / (DIFF_HD // 4)))
    row_ang = np.floor(idx / GRID_W)[:, None] * ax_freq[None, :]
    col_ang = (idx % GRID_W)[:, None] * ax_freq[None, :]
    is_row = ((lane % 64) < 32)[None, :]
    diff = pack(np.where(is_row, row_ang[:, lane % 16], col_ang[:, lane % 16]))
    return ret + diff


def kernel(x, c, ctx, c_ctx, w_ada, b_ada, norm1_w, w_in, ret_decay_logit, gla_w_gate, gla_b_gate,
           gla_norm_w, diff_lambda, diff_norm_w, w_out, norm2_w, w_ffn_in, w_ffn_out, final_norm_w):
    batch, n_lat, _ = x.shape
    assert ctx.shape[1] == CTX and batch < 16

    w_in_b = w_in.astype(BF16)
    w_out_b = w_out.astype(BF16)
    w_ffn_in_b = w_ffn_in.astype(BF16)
    w_ffn_out_b = w_ffn_out.astype(BF16)
    zpad = jnp.zeros((DEPTH, GLA_RANK, NH * GLA_DK), F32)
    wgf = jnp.concatenate([gla_w_gate[:, 0], zpad], axis=1).astype(BF16)
    wgb = jnp.concatenate([zpad, gla_w_gate[:, 1]], axis=1).astype(BF16)
    gla_nw = jnp.tile(gla_norm_w, (1, NH)).reshape(DEPTH, 1, NH * GLA_DV)
    norm1 = norm1_w.reshape(DEPTH, 1, D)
    norm2 = norm2_w.reshape(DEPTH, 1, D)
    diff_nw = diff_norm_w.reshape(DEPTH, 1, DIFF_DV)
    final_w = final_norm_w.reshape(1, D)
    tabs = _rope_tables(n_lat)

    cond = jnp.concatenate([c, c_ctx[None, :], jnp.zeros((16 - batch - 1, D), F32)], axis=0)
    mods = _ada(cond, w_ada, b_ada).reshape(DEPTH * 16 * 6, 1, D)

    t_all = CTX + n_lat
    stream = (ctx, x)
    for layer in range(DEPTH):
        final = layer == DEPTH - 1
        til = _Tiling(*POST_TILING["first" if layer == 0 else "last" if final else "middle"])
        rq, rk, rv, rg, gq, gk, gv, gr, dq, dk, dv, lr = _proj(
            layer, batch, t_all, stream, mods, norm1, w_in_b, tabs)
        ro, go = _mix(layer, rq, rk, rv, rg, gq, gk, gv, lr, gr, ret_decay_logit, wgf, wgb, gla_b_gate, gla_nw)
        do = _attn(layer, dq, dk, dv, diff_lambda, diff_nw)
        stream = _post(layer, batch, t_all, stream, ro, go, do, mods, w_out_b, norm2, w_ffn_in_b, w_ffn_out_b,
                       final_w, til, final)
    return stream
```

```python
import functools
import math

import jax
import jax.numpy as jnp
import numpy as np
from jax import lax
from jax.experimental import pallas as pl
from jax.experimental.pallas import tpu as pltpu

F32 = jnp.float32
BF16 = jnp.bfloat16

D = 1024
DEPTH = 4
CTX = 256
GRID_W = 64
EPS = 1e-6
GN_EPS = 1e-5
ROPE_BASE = 10000.0

NH = 4
RET_DK, RET_DV = 32, 64
GLA_DK, GLA_DV = 32, 64
GLA_RANK = 16
GLA_TAU = 16.0
DIFF_HD, DIFF_DV = 64, 128
D_FF = 2816

O_RQ, O_RK, O_RV, O_RG = 0, 128, 256, 512
O_GQ, O_GK, O_GV, O_GR = 768, 896, 1024, 1280
O_LR = 1536
N_PROJ_A = O_LR + 2 * GLA_RANK

LANE = 128
PROJ_SUBTILES = 3
POST_TILING = {"first": (256, 3, 0), "middle": (288, 2, 0), "last": (256, 2, 1)}
TQ = 256
N_CAST = 3
RET_C = 128
GLA_C = 128
CHUNK_GROUP = 9
VMEM_LIMIT = 56 * 1024 * 1024


def _mm(a, b):
    return jnp.dot(a, b, preferred_element_type=F32)


def _mm_nt(a, b):
    return lax.dot_general(a, b, (((1,), (1,)), ((), ())), preferred_element_type=F32)


def _mm_tn(a, b):
    return lax.dot_general(a, b, (((0,), (0,)), ((), ())), preferred_element_type=F32)


def _split(x):
    hi = x.astype(BF16)
    lo = (x - hi.astype(F32)).astype(BF16)
    return hi, lo


def _hilo_mm(m_bf16, x):
    hi, lo = _split(x)
    n = x.shape[1]
    r = _mm(m_bf16, jnp.concatenate([hi, lo], axis=1))
    return r[:, :n] + r[:, n:]


def _group_mean(x, avg_bf16):
    hi, lo = _split(x)
    m = x.shape[0]
    r = _mm(jnp.concatenate([hi, lo], axis=0), avg_bf16)
    return r[:m] + r[m:]


def _log_sigmoid(z):
    return jnp.minimum(z, 0.0) - jnp.log(1.0 + jnp.exp(-jnp.abs(z)))


def _silu(z):
    return z / (1.0 + jnp.exp(-z))


def _iota(shape, dim):
    return lax.broadcasted_iota(jnp.int32, shape, dim)


def _head_masks(rows, lanes, head_shift):
    head = _iota((rows, lanes), 1) >> head_shift
    return [head == h for h in range(NH)]


def _stack_heads(x, masks):
    x = x.astype(F32)
    return jnp.concatenate([jnp.where(m, x, 0.0).astype(BF16) for m in masks], axis=0)


def _group_avg_matrix(n, group_shift):
    r = _iota((n, n), 0) >> group_shift
    c = _iota((n, n), 1) >> group_shift
    return jnp.where(r == c, 1.0 / (1 << group_shift), 0.0).astype(BF16)


def _ada_kernel(cond_ref, w_ref, b_ref, o_ref):
    cond = _silu(cond_ref[...])
    w = w_ref[0]
    c_hi, c_lo = _split(cond)
    w_hi, w_lo = _split(w)
    hh = _mm(jnp.concatenate([c_hi, c_lo], axis=0), w_hi)
    o_ref[0] = hh[:16] + hh[16:] + _mm(c_hi, w_lo) + b_ref[0]


def _ada(cond, w_ada, b_ada):
    bn = 1536
    n = w_ada.shape[-1]
    return pl.pallas_call(
        _ada_kernel,
        grid=(DEPTH, n // bn),
        in_specs=[
            pl.BlockSpec((16, D), lambda l, j: (0, 0)),
            pl.BlockSpec((1, D, bn), lambda l, j: (l, 0, j)),
            pl.BlockSpec((1, 1, bn), lambda l, j: (l, 0, j)),
        ],
        out_specs=pl.BlockSpec((1, 16, bn), lambda l, j: (l, 0, j)),
        out_shape=jax.ShapeDtypeStruct((DEPTH, 16, n), F32),
        compiler_params=pltpu.CompilerParams(dimension_semantics=("arbitrary", "arbitrary")),
        name="ada",
    )(cond, w_ada, b_ada.reshape(DEPTH, 1, n))


def _mod_specs(layer, j, batch):
    lat = pl.BlockSpec((None, 1, D), lambda b, t: ((layer * 16 + b) * 6 + j, 0, 0))
    ctx = pl.BlockSpec((None, 1, D), lambda b, t: ((layer * 16 + batch) * 6 + j, 0, 0))
    return [lat, ctx]


class _Tiling:
    def __init__(self, sub, n_sub, s0=0):
        self.sub, self.n_sub, self.s0 = sub, n_sub, s0
        self.rows = sub * n_sub
        self.spans = [(j * sub, (j + 1) * sub) for j in range(n_sub)]

    def s(self, j, t=None):
        return (pl.program_id(1) if t is None else t) * self.n_sub + j + self.s0

    def in_specs(self, arr, width):
        specs, ops = [], []
        for j in range(self.n_sub):
            if isinstance(arr, tuple):
                assert CTX % self.sub == 0
                n_ctx = CTX // self.sub
                specs += [
                    pl.BlockSpec((1, self.sub, width), lambda b, t, j=j: (b, jnp.minimum(self.s(j, t), n_ctx - 1), 0)),
                    pl.BlockSpec((1, self.sub, width), lambda b, t, j=j: (b, jnp.maximum(self.s(j, t) - n_ctx, 0), 0))]
                ops += list(arr)
            else:
                specs.append(pl.BlockSpec((1, self.sub, width), lambda b, t, j=j: (b, self.s(j, t), 0)))
                ops.append(arr)
        return specs, ops

    def load(self, refs, j):
        if len(refs) == 2 * self.n_sub:
            return jnp.where(self.s(j) < CTX // self.sub, refs[2 * j][0], refs[2 * j + 1][0])
        return refs[j][0]

    def mod(self, lat_ref, ctx_ref, j):
        if CTX % self.sub == 0:
            return jnp.where(self.s(j) < CTX // self.sub, ctx_ref[...], lat_ref[...])
        assert self.sub > CTX and self.s0 == 0
        if j > 0:
            return lat_ref[...]
        ctx_rows = (_iota((self.sub, 1), 0) < CTX) & (pl.program_id(1) == 0)
        return jnp.where(ctx_rows, ctx_ref[...], lat_ref[...])


def _rope(x, cos, sa, sb):
    outs = []
    for j in range(x.shape[1] // LANE):
        blk = x[:, j * LANE:(j + 1) * LANE]
        fwd = pltpu.roll(blk, LANE - 16, 1)
        bwd = pltpu.roll(blk, 16, 1)
        outs.append(blk * cos + fwd * sa + bwd * sb)
    return outs[0] if len(outs) == 1 else jnp.concatenate(outs, axis=1)


def _proj_kernel(*refs, n_stream, til):
    (sh_l, sh_c, sc_l, sc_c, nw_ref, w_ref, rcos, rsa, rsb, dcos, dsa, dsb,
     rq_o, rk_o, rv_o, rg_o, gq_o, gk_o, gv_o, gr_o, dq_o, dk_o, dv_o, lr_o, wa_ref, wd_ref) = refs[n_stream:]

    @pl.when((pl.program_id(0) == 0) & (pl.program_id(1) == 0))
    def _():
        wa_ref[...] = w_ref[0, :, 0:N_PROJ_A]
        wd_ref[...] = w_ref[0, :, N_PROJ_A:]

    wa, wd = wa_ref[...], wd_ref[...]
    h = []
    for j in range(til.n_sub):
        xj = til.load(refs[:n_stream], j)
        y = xj * lax.rsqrt(jnp.mean(xj * xj, axis=-1, keepdims=True) + EPS) * nw_ref[...]
        h.append((y * (1.0 + til.mod(sc_l, sc_c, j)) + til.mod(sh_l, sh_c, j)).astype(BF16))
    p = [(_mm(hj, wa), _mm(hj, wd)) for hj in h]

    avg = _group_avg_matrix(NH * RET_DV, 6)
    for (r0, r1), (pa, pd) in zip(til.spans, p):
        rc, ra, rb = rcos[r0:r1, :], rsa[r0:r1, :], rsb[r0:r1, :]
        rq_o[0, r0:r1, :] = _rope(pa[:, O_RQ:O_RQ + 128], rc, ra, rb).astype(BF16)
        rk_o[0, r0:r1, :] = (_rope(pa[:, O_RK:O_RK + 128], rc, ra, rb) * (RET_DK ** -0.5)).astype(BF16)
        rv = pa[:, O_RV:O_RV + 256]
        rv_o[0, r0:r1, :] = (rv - _group_mean(rv, avg)).astype(BF16)
        rg_o[0, r0:r1, :] = pa[:, O_RG:O_RG + 256]
        gq_o[0, r0:r1, :] = (pa[:, O_GQ:O_GQ + 128] * (GLA_DK ** -0.5)).astype(BF16)
        gk_o[0, r0:r1, :] = pa[:, O_GK:O_GK + 128].astype(BF16)
        gv_o[0, r0:r1, :] = pa[:, O_GV:O_GV + 256].astype(BF16)
        gr_o[0, r0:r1, :] = pa[:, O_GR:O_GR + 256]
        lr_o[0, r0:r1, :] = pa[:, O_LR:O_LR + 2 * GLA_RANK].astype(BF16)
        dc, da, db = dcos[r0:r1, :], dsa[r0:r1, :], dsb[r0:r1, :]
        dq_o[0, r0:r1, :] = (_rope(pd[:, 0:512], dc, da, db)
                             * (DIFF_HD ** -0.5 * math.log2(math.e))).astype(BF16)
        dk_o[0, r0:r1, :] = _rope(pd[:, 512:1024], dc, da, db).astype(BF16)
        dv_o[0, r0:r1, :] = pd[:, 1024:1536].astype(BF16)


def _proj(layer, batch, t_all, stream, mods, norm1_w, w_in_b, tabs):
    til = _Tiling(CTX, PROJ_SUBTILES)
    tm = til.rows
    tok = lambda n: pl.BlockSpec((1, tm, n), lambda b, t: (b, t, 0))
    tab = pl.BlockSpec((tm, LANE), lambda b, t: (t, 0))
    widths = [128, 128, 256, 256, 128, 128, 256, 256, 512, 512, 512, 2 * GLA_RANK]
    dtypes = [BF16, BF16, BF16, F32, BF16, BF16, BF16, F32, BF16, BF16, BF16, BF16]
    stream_specs, stream_ops = til.in_specs(stream, D)
    n_proj = w_in_b.shape[-1]
    return pl.pallas_call(
        functools.partial(_proj_kernel, n_stream=len(stream_ops), til=til),
        grid=(batch, t_all // tm),
        in_specs=stream_specs + _mod_specs(layer, 0, batch) + _mod_specs(layer, 1, batch) + [
            pl.BlockSpec((None, 1, D), lambda b, t: (layer, 0, 0)),
            pl.BlockSpec((1, D, n_proj), lambda b, t: (layer, 0, 0), pipeline_mode=pl.Buffered(1)),
            tab, tab, tab, tab, tab, tab,
        ],
        out_specs=[tok(n) for n in widths],
        out_shape=[jax.ShapeDtypeStruct((batch, t_all, n), dt) for n, dt in zip(widths, dtypes)],
        scratch_shapes=[pltpu.VMEM((D, N_PROJ_A), BF16), pltpu.VMEM((D, n_proj - N_PROJ_A), BF16)],
        compiler_params=pltpu.CompilerParams(
            dimension_semantics=("arbitrary", "arbitrary"), vmem_limit_bytes=VMEM_LIMIT),
        name="proj",
    )(*stream_ops, mods, mods, mods, mods, norm1_w, w_in_b, *tabs)


def _ret_phases(logit_ref, q_ref, k_ref, v_ref, g_ref, o_ref, ut, st):
    n_chunks = q_ref.shape[1] // RET_C
    n_ctx = CTX // RET_C
    c = RET_C

    lane_head = _iota((c, LANE), 1) >> 5
    i_f = _iota((c, LANE), 0).astype(F32)

    def lanes_of(d):
        z = jnp.zeros((c, LANE), F32)
        for h in range(NH):
            z = jnp.where(lane_head == h, logit_ref[d, h], z)
        return _log_sigmoid(z)

    lg_f, lg_b = lanes_of(0), lanes_of(1)
    kdec_f = jnp.exp((c - 1.0 - i_f) * lg_f)
    kdec_b = jnp.exp(i_f * lg_b)
    qdec_f = jnp.exp((i_f + 1.0) * lg_f)
    qdec_b = jnp.exp((c - i_f) * lg_b)
    cd_f = jnp.exp(c * lg_f[0:1])
    cd_b = jnp.exp(c * lg_b[0:1])
    kdec = jnp.concatenate([kdec_f, kdec_b], axis=1)
    qdec = jnp.concatenate([qdec_f, qdec_b], axis=1)

    nd = NH * RET_DV
    bd = (_iota((nd, 2 * LANE), 0) >> 6) == ((_iota((nd, 2 * LANE), 1) & (LANE - 1)) >> 5)

    def p0(i):
        for j in range(CHUNK_GROUP):
            n = i * CHUNK_GROUP + j
            rows = pl.ds(pl.multiple_of(n * c, c), c)
            k = k_ref[0, rows, :].astype(F32)
            k2 = (jnp.concatenate([k, k], axis=1) * kdec).astype(BF16)
            ut[n] = jnp.where(bd, _mm_tn(v_ref[0, rows, :], k2), 0.0)
        yield

    def scan():
        s = jnp.zeros((nd, LANE), F32)
        for n in range(n_chunks):
            st[n, :, 0:LANE] = s.astype(BF16)
            s = s * cd_f + ut[n, :, 0:LANE]
        s = jnp.zeros((nd, LANE), F32)
        for n in list(range(n_ctx - 1, -1, -1)) + list(range(n_chunks - 1, n_ctx - 1, -1)):
            st[n, :, LANE:] = s.astype(BF16)
            s = s * cd_b + ut[n, :, LANE:]

    rel = (_iota((c, c), 0) - _iota((c, c), 1)).astype(F32)
    eye = jnp.where(rel == 0.0, 1.0, 0.0)
    dmats = []
    for h in range(NH):
        lf = _log_sigmoid(jnp.full((c, c), logit_ref[0, h], F32))
        lb = _log_sigmoid(jnp.full((c, c), logit_ref[1, h], F32))
        dmats.append(jnp.exp(jnp.where(rel >= 0.0, rel * lf, -rel * lb)) + eye)
    dwide = jnp.concatenate(dmats, axis=1)
    kmasks, vmasks = _head_masks(c, LANE, 5), _head_masks(c, NH * RET_DV, 6)
    avg = _group_avg_matrix(NH * RET_DV, 6)

    def p2(i):
        ns = [i * CHUNK_GROUP + j for j in range(CHUNK_GROUP)]
        rows = [pl.ds(pl.multiple_of(n * c, c), c) for n in ns]
        sc, inter = [], []
        for n, r in zip(ns, rows):
            q = q_ref[0, r, :]
            sc.append(_mm_nt(q, _stack_heads(k_ref[0, r, :], kmasks)))
            qf = q.astype(F32)
            inter.append(_mm_nt((jnp.concatenate([qf, qf], axis=1) * qdec).astype(BF16), st[n]))
        yield
        o = []
        for r, s_c, o_c in zip(rows, sc, inter):
            o.append(_mm((s_c * dwide).astype(BF16), _stack_heads(v_ref[0, r, :], vmasks)) + o_c)
        yield
        var = [_mm((o_c * o_c).astype(BF16), avg) for o_c in o]
        yield
        for r, o_c, v_c in zip(rows, o, var):
            o_ref[0, r, :] = (_silu(g_ref[0, r, :]) * o_c * lax.rsqrt(v_c + GN_EPS)).astype(BF16)
        yield

    return p0, scan, p2


def _gla_phases(q_ref, k_ref, v_ref, lr_ref, g_ref, wgf_ref, wgb_ref, bg_ref, nw_ref, o_ref,
                cf_s, cb_s, ut, dec, st):
    c = GLA_C
    n_chunks = q_ref.shape[1] // c
    n_ctx = CTX // c
    nd = NH * GLA_DV

    ltri = jnp.where(_iota((c, c), 0) >= _iota((c, c), 1), 1.0, 0.0)
    utri = jnp.where(_iota((c, c), 0) <= _iota((c, c), 1), 1.0, 0.0)
    ltri_b, utri_b = ltri.astype(BF16), utri.astype(BF16)
    bd = (_iota((nd, 2 * LANE), 0) >> 6) == ((_iota((nd, 2 * LANE), 1) & (LANE - 1)) >> 5)

    lr = lr_ref[0]
    cf_s[...] = _log_sigmoid(_mm(lr, wgf_ref[...]) + bg_ref[0:1, :]) * (1.0 / GLA_TAU)
    cb_s[...] = _log_sigmoid(_mm(lr, wgb_ref[...]) + bg_ref[1:2, :]) * (1.0 / GLA_TAU)

    def p0(i):
        ns = [i * CHUNK_GROUP + j for j in range(CHUNK_GROUP)]
        rows = [pl.ds(pl.multiple_of(n * c, c), c) for n in ns]
        cf = [_hilo_mm(ltri_b, cf_s[r, :]) for r in rows]
        cb = [_hilo_mm(utri_b, cb_s[r, :]) for r in rows]
        yield
        for n, r, cf_c, cb_c in zip(ns, rows, cf, cb):
            cf_s[r, :] = cf_c
            cb_s[r, :] = cb_c
            cf_end, cb_end = cf_c[c - 1:c, :], cb_c[0:1, :]
            k = k_ref[0, r, :].astype(F32)
            k2 = jnp.concatenate([k * jnp.exp(cf_end - cf_c), k * jnp.exp(cb_end - cb_c)], axis=1).astype(BF16)
            ut[n] = jnp.where(bd, _mm_tn(v_ref[0, r, :], k2), 0.0)
            dec[n] = jnp.broadcast_to(jnp.exp(jnp.concatenate([cf_end, cb_end], axis=1)), (8, 2 * LANE))
        yield

    def scan():
        def scan_f(n, s):
            st[n, :, 0:LANE] = s.astype(BF16)
            return s * dec[n, 0:1, 0:LANE] + ut[n, :, 0:LANE]

        def scan_b(i, s, hi):
            n = hi - 1 - i
            st[n, :, LANE:] = s.astype(BF16)
            return s * dec[n, 0:1, LANE:] + ut[n, :, LANE:]

        zero = jnp.zeros((nd, LANE), F32)
        lax.fori_loop(0, n_chunks, scan_f, zero)
        s_ctx = lax.fori_loop(0, n_ctx, functools.partial(scan_b, hi=n_ctx), zero)
        lax.fori_loop(0, n_chunks - n_ctx, functools.partial(scan_b, hi=n_chunks), s_ctx)

    avg = _group_avg_matrix(nd, 6)
    nw = nw_ref[...]
    ltri_w = jnp.concatenate([ltri] * NH, axis=1)
    utri_w = jnp.concatenate([utri] * NH, axis=1)

    kmasks, vmasks = _head_masks(c, LANE, 5), _head_masks(c, nd, 6)

    def p2(i):
        ns = [i * CHUNK_GROUP + j for j in range(CHUNK_GROUP)]
        rows = [pl.ds(pl.multiple_of(n * c, c), c) for n in ns]
        s_fb, inter = [], []
        for n, r in zip(ns, rows):
            q = q_ref[0, r, :].astype(F32)
            k = k_ref[0, r, :].astype(F32)
            cf, cb = cf_s[r, :], cb_s[r, :]
            rf, rb = cf[c // 2 - 1:c // 2, :], cb[c // 2:c // 2 + 1, :]
            qf, qb = (q * jnp.exp(cf - rf)).astype(BF16), (q * jnp.exp(cb - rb)).astype(BF16)
            zq = jnp.zeros_like(qf)
            q_fb = jnp.concatenate([jnp.concatenate([qf, zq], axis=1), jnp.concatenate([zq, qb], axis=1)], axis=0)
            k_fb = jnp.concatenate([_stack_heads(k * jnp.exp(rf - cf), kmasks),
                                    _stack_heads(k * jnp.exp(rb - cb), kmasks)], axis=1)
            s_fb.append(_mm_nt(q_fb, k_fb))
            q2 = jnp.concatenate([q * jnp.exp(cf), q * jnp.exp(cb)], axis=1).astype(BF16)
            inter.append(_mm_nt(q2, st[n]))
        yield
        o = []
        for r, s_c, o_c in zip(rows, s_fb, inter):
            a = (s_c[:c] * ltri_w + s_c[c:] * utri_w).astype(BF16)
            o.append(_mm(a, _stack_heads(v_ref[0, r, :], vmasks)) + o_c)
        yield
        ms = [_mm((o_c * o_c).astype(BF16), avg) for o_c in o]
        yield
        for r, o_c, ms_c in zip(rows, o, ms):
            y = o_c * lax.rsqrt(ms_c + EPS) * nw
            o_ref[0, r, :] = (_silu(g_ref[0, r, :]) * y).astype(BF16)
        yield

    return p0, scan, p2


def _mix_kernel(logit_ref, rq, rk, rv, rg, gq, gk, gv, lr, gg, wgf, wgb, bg, nw, ro, go,
                r_ut, r_st, cf_s, cb_s, g_ut, g_dec, g_st):
    mixers = [_ret_phases(logit_ref, rq, rk, rv, rg, ro, r_ut, r_st),
              _gla_phases(gq, gk, gv, lr, gg, wgf, wgb, bg, nw, go, cf_s, cb_s, g_ut, g_dec, g_st)]
    n_groups = rq.shape[1] // (RET_C * CHUNK_GROUP)

    def lockstep(phase, i):
        done = object()
        gens = [m[phase](i) for m in mixers]
        while gens:
            gens = [g for g in gens if next(g, done) is not done]

    def p0(i, carry):
        lockstep(0, i)
        return carry

    def p2(i, carry):
        lockstep(2, i)
        return carry

    lax.fori_loop(0, n_groups, p0, 0)
    for m in mixers:
        m[1]()
    lax.fori_loop(0, n_groups, p2, 0)


def _mix(layer, rq, rk, rv, rg, gq, gk, gv, lr, gg, ret_decay_logit, wgf, wgb, bg, nw):
    batch, t_all, _ = rq.shape
    assert RET_C == GLA_C and t_all % (RET_C * CHUNK_GROUP) == 0
    n_chunks = t_all // RET_C
    nd = NH * GLA_DV
    tok = lambda n: pl.BlockSpec((1, t_all, n), lambda b, lg: (b, 0, 0))
    par = lambda r, n: pl.BlockSpec((None, r, n), lambda b, lg: (layer, 0, 0))
    state = lambda dt: pltpu.VMEM((n_chunks, nd, 2 * LANE), dt)
    return pl.pallas_call(
        _mix_kernel,
        grid_spec=pltpu.PrefetchScalarGridSpec(
            num_scalar_prefetch=1,
            grid=(batch,),
            in_specs=[tok(128), tok(128), tok(256), tok(256),
                      tok(128), tok(128), tok(256), tok(2 * GLA_RANK), tok(256),
                      par(2 * GLA_RANK, 128), par(2 * GLA_RANK, 128), par(2, 128), par(1, 256)],
            out_specs=[tok(256), tok(256)],
            scratch_shapes=[
                state(F32), state(BF16),
                pltpu.VMEM((t_all, LANE), F32), pltpu.VMEM((t_all, LANE), F32),
                state(F32), pltpu.VMEM((n_chunks, 8, 2 * LANE), F32), state(BF16),
            ],
        ),
        out_shape=[jax.ShapeDtypeStruct((batch, t_all, 256), BF16)] * 2,
        compiler_params=pltpu.CompilerParams(
            dimension_semantics=("arbitrary",), vmem_limit_bytes=VMEM_LIMIT),
        name="mix",
    )(ret_decay_logit[layer], rq, rk, rv, rg, gq, gk, gv, lr, gg, wgf, wgb, bg, nw)


def _attn_kernel(q_ref, k_ref, v_ref, lam_ref, nw_ref, *refs, lam_init):
    w_f32, (o_ref, *w_bf16), vx_ref = refs[:N_CAST], refs[N_CAST:2 * N_CAST + 1], refs[-1]
    for src, dst in zip(w_f32, w_bf16):
        dst[...] = src[0].astype(BF16)
    t_all = k_ref.shape[1]
    vx_ref[:, 0:DIFF_DV] = v_ref[0]
    vx_ref[:, DIFF_DV:] = jnp.ones((t_all, DIFF_DV), BF16)
    lp = lam_ref[...]
    lam = (jnp.exp(jnp.sum(lp[0:1] * lp[1:2], axis=-1, keepdims=True))
           - jnp.exp(jnp.sum(lp[2:3] * lp[3:4], axis=-1, keepdims=True)) + lam_init)
    first = _iota((TQ, LANE), 1) < DIFF_HD

    assert CTX == TQ
    tiles = [(r, CTX if r == 0 else t_all) for r in range(0, t_all, TQ)]
    s = []
    for r, n_keys in tiles:
        q = q_ref[0, r:r + TQ, :]
        for qh in (jnp.where(first, q, jnp.zeros_like(q)), jnp.where(first, jnp.zeros_like(q), q)):
            s.append(_mm_nt(qh, k_ref[0, 0:n_keys, :]).astype(BF16))
    e = [jnp.exp2(s_h - jnp.max(s_h, axis=-1, keepdims=True)) for s_h in s]
    ol = [_mm(e_h, vx_ref[0:e_h.shape[1], :]) for e_h in e]
    for j, (r, _) in enumerate(tiles):
        o1, o2 = (x[:, 0:DIFF_DV] / x[:, DIFF_DV:] for x in ol[2 * j:2 * j + 2])
        o = o1 - lam * o2
        y = o * lax.rsqrt(jnp.mean(o * o, axis=-1, keepdims=True) + EPS) * nw_ref[...]
        o_ref[0, r:r + TQ, :] = (y * (1.0 - lam_init)).astype(BF16)


def _attn(layer, dq, dk, dv, diff_lambda, diff_norm_w, weights_f32):
    batch, t_all, _ = dq.shape
    assert len(weights_f32) == N_CAST
    lam_init = 0.8 - 0.6 * math.exp(-0.3 * layer)
    head = pl.BlockSpec((1, t_all, LANE), lambda b, h: (b, 0, h))

    def slab(w):
        n_steps = max(n for n in range(1, batch * NH + 1) if w.shape[1] % (16 * n) == 0)
        return w.shape[1] // n_steps, lambda b, h: jnp.minimum(b * NH + h, n_steps - 1)

    slabs = [slab(w) for w in weights_f32]
    out = pl.pallas_call(
        functools.partial(_attn_kernel, lam_init=lam_init),
        grid=(batch, NH),
        in_specs=[
            head, head, head,
            pl.BlockSpec((None, 4, DIFF_HD), lambda b, h: (layer, 0, 0)),
            pl.BlockSpec((None, 1, DIFF_DV), lambda b, h: (layer, 0, 0)),
        ] + [pl.BlockSpec((1, r, w.shape[-1]), lambda b, h, i=i: (layer, i(b, h), 0))
             for w, (r, i) in zip(weights_f32, slabs)],
        out_specs=[head] + [pl.BlockSpec((r, w.shape[-1]), lambda b, h, i=i: (i(b, h), 0))
                            for w, (r, i) in zip(weights_f32, slabs)],
        out_shape=[jax.ShapeDtypeStruct((batch, t_all, NH * DIFF_DV), BF16)]
        + [jax.ShapeDtypeStruct(w.shape[1:], BF16) for w in weights_f32],
        scratch_shapes=[pltpu.VMEM((t_all, 2 * DIFF_DV), BF16)],
        compiler_params=pltpu.CompilerParams(
            dimension_semantics=("arbitrary", "arbitrary"), vmem_limit_bytes=VMEM_LIMIT),
        name="attn",
    )(dq, dk, dv, diff_lambda, diff_norm_w, *weights_f32)
    return out[0], out[1:]


def _post_kernel(*refs, n_stream, final, til):
    n = til.n_sub
    x_refs = refs[:n_stream]
    ro_refs, go_refs, do_refs = (refs[n_stream + i * n:n_stream + (i + 1) * n] for i in range(3))
    (g1_l, g1_c, sh_l, sh_c, sc_l, sc_c, g2_l, g2_c,
     wo_ref, nw_ref, wi_ref, wf_ref, fw_ref, o_ref) = refs[n_stream + 3 * n:]

    def rms(v, w_ref):
        return v * lax.rsqrt(jnp.mean(v * v, axis=-1, keepdims=True) + EPS) * w_ref[...]

    subs = range(n)
    att = [_mm(jnp.concatenate([ro_refs[j][0], go_refs[j][0], do_refs[j][0]], axis=-1), wo_ref[0]) for j in subs]
    xm = [til.load(x_refs, j) + til.mod(g1_l, g1_c, j) * att[j] for j in subs]
    h = [(rms(xm[j], nw_ref) * (1.0 + til.mod(sc_l, sc_c, j)) + til.mod(sh_l, sh_c, j)).astype(BF16) for j in subs]
    gu = [_mm(h[j], wi_ref[0]) for j in subs]
    ff = [_mm((_silu(g[:, :D_FF]) * g[:, D_FF:]).astype(BF16), wf_ref[0]) for g in gu]
    for j, (r0, r1) in enumerate(til.spans):
        xo = xm[j] + til.mod(g2_l, g2_c, j) * ff[j]
        o_ref[0, r0:r1, :] = rms(xo, fw_ref) if final else xo


def _post(layer, batch, t_all, stream, ro, go, do, mods, w_out, norm2_w, w_ffn_in, w_ffn_out, final_w, til, final):
    n_rows = t_all - til.s0 * til.sub
    wspec = lambda r, n: pl.BlockSpec((1, r, n), lambda b, t: (0, 0, 0), pipeline_mode=pl.Buffered(1))
    stream_specs, stream_ops = til.in_specs(stream, D)
    tok_specs, tok_ops = zip(*(til.in_specs(a, a.shape[-1]) for a in (ro, go, do)))
    mod_specs = sum((_mod_specs(layer, j, batch) for j in (2, 3, 4, 5)), [])
    return pl.pallas_call(
        functools.partial(_post_kernel, n_stream=len(stream_ops), final=final, til=til),
        grid=(batch, n_rows // til.rows),
        in_specs=stream_specs + sum(tok_specs, []) + mod_specs + [
            wspec(D, D),
            pl.BlockSpec((None, 1, D), lambda b, t: (layer, 0, 0)),
            wspec(D, 2 * D_FF), wspec(D_FF, D),
            pl.BlockSpec((1, D), lambda b, t: (0, 0)),
        ],
        out_specs=pl.BlockSpec((1, til.rows, D), lambda b, t: (b, t, 0)),
        out_shape=jax.ShapeDtypeStruct((batch, n_rows, D), F32),
        compiler_params=pltpu.CompilerParams(
            dimension_semantics=("arbitrary", "arbitrary"), vmem_limit_bytes=VMEM_LIMIT),
        name="post",
    )(*stream_ops, *sum(tok_ops, []), *([mods] * 8), w_out, norm2_w, w_ffn_in, w_ffn_out, final_w)


def _rope_tables(n_lat):
    lane = np.arange(LANE)
    first = ((lane % 32) < 16)[None, :]

    def pack(ang):
        cos, sin = np.cos(ang), np.sin(ang)
        tabs = (cos, np.where(first, -sin, 0.0), np.where(first, 0.0, sin))
        ident = (np.ones((CTX, LANE)), np.zeros((CTX, LANE)), np.zeros((CTX, LANE)))
        return tuple(jnp.asarray(np.concatenate([i, t], axis=0), dtype=F32) for i, t in zip(ident, tabs))

    idx = np.arange(n_lat, dtype=np.float64)
    ret_freq = 1.0 / (ROPE_BASE ** np.linspace(0.0, 1.0, RET_DK // 2))
    ret = pack((idx[:, None] * ret_freq[None, :])[:, lane % 16])
    ax_freq = 1.0 / (ROPE_BASE ** (np.arange(DIFF_HD // 4) / (DIFF_HD // 4)))
    row_ang = np.floor(idx / GRID_W)[:, None] * ax_freq[None, :]
    col_ang = (idx % GRID_W)[:, None] * ax_freq[None, :]
    is_row = ((lane % 64) < 32)[None, :]
    diff = pack(np.where(is_row, row_ang[:, lane % 16], col_ang[:, lane % 16]))
    return ret + diff


def kernel(x, c, ctx, c_ctx, w_ada, b_ada, norm1_w, w_in, ret_decay_logit, gla_w_gate, gla_b_gate,
           gla_norm_w, diff_lambda, diff_norm_w, w_out, norm2_w, w_ffn_in, w_ffn_out, final_norm_w):
    batch, n_lat, _ = x.shape
    assert ctx.shape[1] == CTX and batch < 16

    w_in_b = w_in.astype(BF16)
    zpad = jnp.zeros((DEPTH, GLA_RANK, NH * GLA_DK), F32)
    wgf = jnp.concatenate([gla_w_gate[:, 0], zpad], axis=1).astype(BF16)
    wgb = jnp.concatenate([zpad, gla_w_gate[:, 1]], axis=1).astype(BF16)
    gla_nw = jnp.tile(gla_norm_w, (1, NH)).reshape(DEPTH, 1, NH * GLA_DV)
    norm1 = norm1_w.reshape(DEPTH, 1, D)
    norm2 = norm2_w.reshape(DEPTH, 1, D)
    diff_nw = diff_norm_w.reshape(DEPTH, 1, DIFF_DV)
    final_w = final_norm_w.reshape(1, D)
    tabs = _rope_tables(n_lat)

    cond = jnp.concatenate([c, c_ctx[None, :], jnp.zeros((16 - batch - 1, D), F32)], axis=0)
    mods = _ada(cond, w_ada, b_ada).reshape(DEPTH * 16 * 6, 1, D)

    t_all = CTX + n_lat
    stream = (ctx, x)
    for layer in range(DEPTH):
        final = layer == DEPTH - 1
        til = _Tiling(*POST_TILING["first" if layer == 0 else "last" if final else "middle"])
        rq, rk, rv, rg, gq, gk, gv, gr, dq, dk, dv, lr = _proj(
            layer, batch, t_all, stream, mods, norm1, w_in_b, tabs)
        ro, go = _mix(layer, rq, rk, rv, rg, gq, gk, gv, lr, gr, ret_decay_logit, wgf, wgb, gla_b_gate, gla_nw)
        do, (wo_b, wi_b, wf_b) = _attn(layer, dq, dk, dv, diff_lambda, diff_nw, (w_out, w_ffn_in, w_ffn_out))
        stream = _post(layer, batch, t_all, stream, ro, go, do, mods, wo_b[None], norm2, wi_b[None], wf_b[None],
                       final_w, til, final)
    return stream
```

```python
import functools
import math

import jax
import jax.numpy as jnp
import numpy as np
from jax import lax
from jax.experimental import pallas as pl
from jax.experimental.pallas import tpu as pltpu

F32 = jnp.float32
BF16 = jnp.bfloat16

D = 1024
DEPTH = 4
CTX = 256
GRID_W = 64
EPS = 1e-6
GN_EPS = 1e-5
ROPE_BASE = 10000.0

NH = 4
RET_DK, RET_DV = 32, 64
GLA_DK, GLA_DV = 32, 64
GLA_RANK = 16
GLA_TAU = 16.0
DIFF_HD, DIFF_DV = 64, 128
D_FF = 2816

O_RQ, O_RK, O_RV, O_RG = 0, 128, 256, 512
O_GQ, O_GK, O_GV, O_GR = 768, 896, 1024, 1280
O_LR = 1536
N_PROJ_A = O_LR + 2 * GLA_RANK

LANE = 128
PROJ_SUBTILES = 3
POST_TILING = {"first": (256, 3, 0), "middle": (288, 2, 0), "last": (256, 2, 1)}
TQ = 256
RET_C = 128
GLA_C = 128
CHUNK_GROUP = 9
VMEM_LIMIT = 56 * 1024 * 1024


def _mm(a, b):
    return jnp.dot(a, b, preferred_element_type=F32)


def _mm_nt(a, b):
    return lax.dot_general(a, b, (((1,), (1,)), ((), ())), preferred_element_type=F32)


def _mm_tn(a, b):
    return lax.dot_general(a, b, (((0,), (0,)), ((), ())), preferred_element_type=F32)


def _split(x):
    hi = x.astype(BF16)
    lo = (x - hi.astype(F32)).astype(BF16)
    return hi, lo


def _hilo_mm(m_bf16, x):
    hi, lo = _split(x)
    n = x.shape[1]
    r = _mm(m_bf16, jnp.concatenate([hi, lo], axis=1))
    return r[:, :n] + r[:, n:]


def _group_mean(x, avg_bf16):
    hi, lo = _split(x)
    m = x.shape[0]
    r = _mm(jnp.concatenate([hi, lo], axis=0), avg_bf16)
    return r[:m] + r[m:]


def _log_sigmoid(z):
    return jnp.minimum(z, 0.0) - jnp.log(1.0 + jnp.exp(-jnp.abs(z)))


def _silu(z):
    return z / (1.0 + jnp.exp(-z))


def _iota(shape, dim):
    return lax.broadcasted_iota(jnp.int32, shape, dim)


def _head_masks(rows, lanes, head_shift):
    head = _iota((rows, lanes), 1) >> head_shift
    return [head == h for h in range(NH)]


def _stack_heads(x, masks):
    x = x.astype(F32)
    return jnp.concatenate([jnp.where(m, x, 0.0).astype(BF16) for m in masks], axis=0)


def _group_avg_matrix(n, group_shift):
    r = _iota((n, n), 0) >> group_shift
    c = _iota((n, n), 1) >> group_shift
    return jnp.where(r == c, 1.0 / (1 << group_shift), 0.0).astype(BF16)


def _ada_kernel(cond_ref, w_ref, b_ref, w_in_ref, o_ref, w_in_b_ref):
    w_in_b_ref[...] = w_in_ref[0].astype(BF16)
    cond = _silu(cond_ref[...])
    w = w_ref[0]
    c_hi, c_lo = _split(cond)
    w_hi, w_lo = _split(w)
    hh = _mm(jnp.concatenate([c_hi, c_lo], axis=0), w_hi)
    o_ref[0] = hh[:16] + hh[16:] + _mm(c_hi, w_lo) + b_ref[0]


def _ada(cond, w_ada, b_ada, w_in):
    bn = 1536
    n = w_ada.shape[-1]
    n_col = n // bn
    rows = D // (DEPTH * n_col)
    assert rows % 16 == 0
    n_proj = w_in.shape[-1]
    return pl.pallas_call(
        _ada_kernel,
        grid=(DEPTH, n_col),
        in_specs=[
            pl.BlockSpec((16, D), lambda l, j: (0, 0)),
            pl.BlockSpec((1, D, bn), lambda l, j: (l, 0, j)),
            pl.BlockSpec((1, 1, bn), lambda l, j: (l, 0, j)),
            pl.BlockSpec((1, rows, n_proj), lambda l, j: (0, l * n_col + j, 0)),
        ],
        out_specs=[pl.BlockSpec((1, 16, bn), lambda l, j: (l, 0, j)),
                   pl.BlockSpec((rows, n_proj), lambda l, j: (l * n_col + j, 0))],
        out_shape=[jax.ShapeDtypeStruct((DEPTH, 16, n), F32), jax.ShapeDtypeStruct((D, n_proj), BF16)],
        compiler_params=pltpu.CompilerParams(dimension_semantics=("arbitrary", "arbitrary")),
        name="ada",
    )(cond, w_ada, b_ada.reshape(DEPTH, 1, n), w_in)


def _mod_specs(layer, j, batch):
    lat = pl.BlockSpec((None, 1, D), lambda b, t: ((layer * 16 + b) * 6 + j, 0, 0))
    ctx = pl.BlockSpec((None, 1, D), lambda b, t: ((layer * 16 + batch) * 6 + j, 0, 0))
    return [lat, ctx]


class _Tiling:
    def __init__(self, sub, n_sub, s0=0):
        self.sub, self.n_sub, self.s0 = sub, n_sub, s0
        self.rows = sub * n_sub
        self.spans = [(j * sub, (j + 1) * sub) for j in range(n_sub)]

    def s(self, j, t=None):
        return (pl.program_id(1) if t is None else t) * self.n_sub + j + self.s0

    def in_specs(self, arr, width):
        specs, ops = [], []
        for j in range(self.n_sub):
            if isinstance(arr, tuple):
                assert CTX % self.sub == 0
                n_ctx = CTX // self.sub
                specs += [
                    pl.BlockSpec((1, self.sub, width), lambda b, t, j=j: (b, jnp.minimum(self.s(j, t), n_ctx - 1), 0)),
                    pl.BlockSpec((1, self.sub, width), lambda b, t, j=j: (b, jnp.maximum(self.s(j, t) - n_ctx, 0), 0))]
                ops += list(arr)
            else:
                specs.append(pl.BlockSpec((1, self.sub, width), lambda b, t, j=j: (b, self.s(j, t), 0)))
                ops.append(arr)
        return specs, ops

    def load(self, refs, j):
        if len(refs) == 2 * self.n_sub:
            return jnp.where(self.s(j) < CTX // self.sub, refs[2 * j][0], refs[2 * j + 1][0])
        return refs[j][0]

    def mod(self, lat_ref, ctx_ref, j):
        if CTX % self.sub == 0:
            return jnp.where(self.s(j) < CTX // self.sub, ctx_ref[...], lat_ref[...])
        assert self.sub > CTX and self.s0 == 0
        if j > 0:
            return lat_ref[...]
        ctx_rows = (_iota((self.sub, 1), 0) < CTX) & (pl.program_id(1) == 0)
        return jnp.where(ctx_rows, ctx_ref[...], lat_ref[...])


def _rope(x, cos, sa, sb):
    outs = []
    for j in range(x.shape[1] // LANE):
        blk = x[:, j * LANE:(j + 1) * LANE]
        fwd = pltpu.roll(blk, LANE - 16, 1)
        bwd = pltpu.roll(blk, 16, 1)
        outs.append(blk * cos + fwd * sa + bwd * sb)
    return outs[0] if len(outs) == 1 else jnp.concatenate(outs, axis=1)


def _proj_kernel(*refs, n_stream, til):
    (sh_l, sh_c, sc_l, sc_c, nw_ref, w_ref, rcos, rsa, rsb, dcos, dsa, dsb,
     rq_o, rk_o, rv_o, rg_o, gq_o, gk_o, gv_o, gr_o, dq_o, dk_o, dv_o, lr_o, wa_ref, wd_ref) = refs[n_stream:]

    @pl.when((pl.program_id(0) == 0) & (pl.program_id(1) == 0))
    def _():
        wa_ref[...] = w_ref[0, :, 0:N_PROJ_A]
        wd_ref[...] = w_ref[0, :, N_PROJ_A:]

    wa, wd = wa_ref[...], wd_ref[...]
    h = []
    for j in range(til.n_sub):
        xj = til.load(refs[:n_stream], j)
        y = xj * lax.rsqrt(jnp.mean(xj * xj, axis=-1, keepdims=True) + EPS) * nw_ref[...]
        h.append((y * (1.0 + til.mod(sc_l, sc_c, j)) + til.mod(sh_l, sh_c, j)).astype(BF16))
    p = [(_mm(hj, wa), _mm(hj, wd)) for hj in h]

    avg = _group_avg_matrix(NH * RET_DV, 6)
    for (r0, r1), (pa, pd) in zip(til.spans, p):
        rc, ra, rb = rcos[r0:r1, :], rsa[r0:r1, :], rsb[r0:r1, :]
        rq_o[0, r0:r1, :] = _rope(pa[:, O_RQ:O_RQ + 128], rc, ra, rb).astype(BF16)
        rk_o[0, r0:r1, :] = (_rope(pa[:, O_RK:O_RK + 128], rc, ra, rb) * (RET_DK ** -0.5)).astype(BF16)
        rv = pa[:, O_RV:O_RV + 256]
        rv_o[0, r0:r1, :] = (rv - _group_mean(rv, avg)).astype(BF16)
        rg_o[0, r0:r1, :] = pa[:, O_RG:O_RG + 256]
        gq_o[0, r0:r1, :] = (pa[:, O_GQ:O_GQ + 128] * (GLA_DK ** -0.5)).astype(BF16)
        gk_o[0, r0:r1, :] = pa[:, O_GK:O_GK + 128].astype(BF16)
        gv_o[0, r0:r1, :] = pa[:, O_GV:O_GV + 256].astype(BF16)
        gr_o[0, r0:r1, :] = pa[:, O_GR:O_GR + 256]
        lr_o[0, r0:r1, :] = pa[:, O_LR:O_LR + 2 * GLA_RANK].astype(BF16)
        dc, da, db = dcos[r0:r1, :], dsa[r0:r1, :], dsb[r0:r1, :]
        dq_o[0, r0:r1, :] = (_rope(pd[:, 0:512], dc, da, db)
                             * (DIFF_HD ** -0.5 * math.log2(math.e))).astype(BF16)
        dk_o[0, r0:r1, :] = _rope(pd[:, 512:1024], dc, da, db).astype(BF16)
        dv_o[0, r0:r1, :] = pd[:, 1024:1536].astype(BF16)


def _proj(layer, batch, t_all, stream, mods, norm1_w, w_in_b, tabs):
    til = _Tiling(CTX, PROJ_SUBTILES)
    tm = til.rows
    tok = lambda n: pl.BlockSpec((1, tm, n), lambda b, t: (b, t, 0))
    tab = pl.BlockSpec((tm, LANE), lambda b, t: (t, 0))
    widths = [128, 128, 256, 256, 128, 128, 256, 256, 512, 512, 512, 2 * GLA_RANK]
    dtypes = [BF16, BF16, BF16, F32, BF16, BF16, BF16, F32, BF16, BF16, BF16, BF16]
    stream_specs, stream_ops = til.in_specs(stream, D)
    n_proj = w_in_b.shape[-1]
    return pl.pallas_call(
        functools.partial(_proj_kernel, n_stream=len(stream_ops), til=til),
        grid=(batch, t_all // tm),
        in_specs=stream_specs + _mod_specs(layer, 0, batch) + _mod_specs(layer, 1, batch) + [
            pl.BlockSpec((None, 1, D), lambda b, t: (layer, 0, 0)),
            pl.BlockSpec((1, D, n_proj), lambda b, t: (0, 0, 0), pipeline_mode=pl.Buffered(1)),
            tab, tab, tab, tab, tab, tab,
        ],
        out_specs=[tok(n) for n in widths],
        out_shape=[jax.ShapeDtypeStruct((batch, t_all, n), dt) for n, dt in zip(widths, dtypes)],
        scratch_shapes=[pltpu.VMEM((D, N_PROJ_A), BF16), pltpu.VMEM((D, n_proj - N_PROJ_A), BF16)],
        compiler_params=pltpu.CompilerParams(
            dimension_semantics=("arbitrary", "arbitrary"), vmem_limit_bytes=VMEM_LIMIT),
        name="proj",
    )(*stream_ops, mods, mods, mods, mods, norm1_w, w_in_b, *tabs)


def _ret_phases(logit_ref, q_ref, k_ref, v_ref, g_ref, o_ref, ut, st):
    n_chunks = q_ref.shape[1] // RET_C
    n_ctx = CTX // RET_C
    c = RET_C

    lane_head = _iota((c, LANE), 1) >> 5
    i_f = _iota((c, LANE), 0).astype(F32)

    def lanes_of(d):
        z = jnp.zeros((c, LANE), F32)
        for h in range(NH):
            z = jnp.where(lane_head == h, logit_ref[d, h], z)
        return _log_sigmoid(z)

    lg_f, lg_b = lanes_of(0), lanes_of(1)
    kdec_f = jnp.exp((c - 1.0 - i_f) * lg_f)
    kdec_b = jnp.exp(i_f * lg_b)
    qdec_f = jnp.exp((i_f + 1.0) * lg_f)
    qdec_b = jnp.exp((c - i_f) * lg_b)
    cd_f = jnp.exp(c * lg_f[0:1])
    cd_b = jnp.exp(c * lg_b[0:1])
    kdec = jnp.concatenate([kdec_f, kdec_b], axis=1)
    qdec = jnp.concatenate([qdec_f, qdec_b], axis=1)

    nd = NH * RET_DV
    bd = (_iota((nd, 2 * LANE), 0) >> 6) == ((_iota((nd, 2 * LANE), 1) & (LANE - 1)) >> 5)

    def p0(i):
        for j in range(CHUNK_GROUP):
            n = i * CHUNK_GROUP + j
            rows = pl.ds(pl.multiple_of(n * c, c), c)
            k = k_ref[0, rows, :].astype(F32)
            k2 = (jnp.concatenate([k, k], axis=1) * kdec).astype(BF16)
            ut[n] = jnp.where(bd, _mm_tn(v_ref[0, rows, :], k2), 0.0)
        yield

    def scan():
        s = jnp.zeros((nd, LANE), F32)
        for n in range(n_chunks):
            st[n, :, 0:LANE] = s.astype(BF16)
            s = s * cd_f + ut[n, :, 0:LANE]
        s = jnp.zeros((nd, LANE), F32)
        for n in list(range(n_ctx - 1, -1, -1)) + list(range(n_chunks - 1, n_ctx - 1, -1)):
            st[n, :, LANE:] = s.astype(BF16)
            s = s * cd_b + ut[n, :, LANE:]

    rel = (_iota((c, c), 0) - _iota((c, c), 1)).astype(F32)
    eye = jnp.where(rel == 0.0, 1.0, 0.0)
    dmats = []
    for h in range(NH):
        lf = _log_sigmoid(jnp.full((c, c), logit_ref[0, h], F32))
        lb = _log_sigmoid(jnp.full((c, c), logit_ref[1, h], F32))
        dmats.append(jnp.exp(jnp.where(rel >= 0.0, rel * lf, -rel * lb)) + eye)
    dwide = jnp.concatenate(dmats, axis=1)
    kmasks, vmasks = _head_masks(c, LANE, 5), _head_masks(c, NH * RET_DV, 6)
    avg = _group_avg_matrix(NH * RET_DV, 6)

    def p2(i):
        ns = [i * CHUNK_GROUP + j for j in range(CHUNK_GROUP)]
        rows = [pl.ds(pl.multiple_of(n * c, c), c) for n in ns]
        sc, inter = [], []
        for n, r in zip(ns, rows):
            q = q_ref[0, r, :]
            sc.append(_mm_nt(q, _stack_heads(k_ref[0, r, :], kmasks)))
            qf = q.astype(F32)
            inter.append(_mm_nt((jnp.concatenate([qf, qf], axis=1) * qdec).astype(BF16), st[n]))
        yield
        o = []
        for r, s_c, o_c in zip(rows, sc, inter):
            o.append(_mm((s_c * dwide).astype(BF16), _stack_heads(v_ref[0, r, :], vmasks)) + o_c)
        yield
        var = [_mm((o_c * o_c).astype(BF16), avg) for o_c in o]
        yield
        for r, o_c, v_c in zip(rows, o, var):
            o_ref[0, r, :] = (_silu(g_ref[0, r, :]) * o_c * lax.rsqrt(v_c + GN_EPS)).astype(BF16)
        yield

    return p0, scan, p2


def _gla_phases(q_ref, k_ref, v_ref, lr_ref, g_ref, wgf_ref, wgb_ref, bg_ref, nw_ref, o_ref,
                cf_s, cb_s, ut, dec, st):
    c = GLA_C
    n_chunks = q_ref.shape[1] // c
    n_ctx = CTX // c
    nd = NH * GLA_DV

    ltri = jnp.where(_iota((c, c), 0) >= _iota((c, c), 1), 1.0, 0.0)
    utri = jnp.where(_iota((c, c), 0) <= _iota((c, c), 1), 1.0, 0.0)
    ltri_b, utri_b = ltri.astype(BF16), utri.astype(BF16)
    bd = (_iota((nd, 2 * LANE), 0) >> 6) == ((_iota((nd, 2 * LANE), 1) & (LANE - 1)) >> 5)

    lr = lr_ref[0]
    cf_s[...] = _log_sigmoid(_mm(lr, wgf_ref[...]) + bg_ref[0:1, :]) * (1.0 / GLA_TAU)
    cb_s[...] = _log_sigmoid(_mm(lr, wgb_ref[...]) + bg_ref[1:2, :]) * (1.0 / GLA_TAU)

    def p0(i):
        ns = [i * CHUNK_GROUP + j for j in range(CHUNK_GROUP)]
        rows = [pl.ds(pl.multiple_of(n * c, c), c) for n in ns]
        cf = [_hilo_mm(ltri_b, cf_s[r, :]) for r in rows]
        cb = [_hilo_mm(utri_b, cb_s[r, :]) for r in rows]
        yield
        for n, r, cf_c, cb_c in zip(ns, rows, cf, cb):
            cf_s[r, :] = cf_c
            cb_s[r, :] = cb_c
            cf_end, cb_end = cf_c[c - 1:c, :], cb_c[0:1, :]
            k = k_ref[0, r, :].astype(F32)
            k2 = jnp.concatenate([k * jnp.exp(cf_end - cf_c), k * jnp.exp(cb_end - cb_c)], axis=1).astype(BF16)
            ut[n] = jnp.where(bd, _mm_tn(v_ref[0, r, :], k2), 0.0)
            dec[n] = jnp.broadcast_to(jnp.exp(jnp.concatenate([cf_end, cb_end], axis=1)), (8, 2 * LANE))
        yield

    def scan():
        def scan_f(n, s):
            st[n, :, 0:LANE] = s.astype(BF16)
            return s * dec[n, 0:1, 0:LANE] + ut[n, :, 0:LANE]

        def scan_b(i, s, hi):
            n = hi - 1 - i
            st[n, :, LANE:] = s.astype(BF16)
            return s * dec[n, 0:1, LANE:] + ut[n, :, LANE:]

        zero = jnp.zeros((nd, LANE), F32)
        lax.fori_loop(0, n_chunks, scan_f, zero)
        s_ctx = lax.fori_loop(0, n_ctx, functools.partial(scan_b, hi=n_ctx), zero)
        lax.fori_loop(0, n_chunks - n_ctx, functools.partial(scan_b, hi=n_chunks), s_ctx)

    avg = _group_avg_matrix(nd, 6)
    nw = nw_ref[...]
    ltri_w = jnp.concatenate([ltri] * NH, axis=1)
    utri_w = jnp.concatenate([utri] * NH, axis=1)

    kmasks, vmasks = _head_masks(c, LANE, 5), _head_masks(c, nd, 6)

    def p2(i):
        ns = [i * CHUNK_GROUP + j for j in range(CHUNK_GROUP)]
        rows = [pl.ds(pl.multiple_of(n * c, c), c) for n in ns]
        s_fb, inter = [], []
        for n, r in zip(ns, rows):
            q = q_ref[0, r, :].astype(F32)
            k = k_ref[0, r, :].astype(F32)
            cf, cb = cf_s[r, :], cb_s[r, :]
            rf, rb = cf[c // 2 - 1:c // 2, :], cb[c // 2:c // 2 + 1, :]
            qf, qb = (q * jnp.exp(cf - rf)).astype(BF16), (q * jnp.exp(cb - rb)).astype(BF16)
            zq = jnp.zeros_like(qf)
            q_fb = jnp.concatenate([jnp.concatenate([qf, zq], axis=1), jnp.concatenate([zq, qb], axis=1)], axis=0)
            k_fb = jnp.concatenate([_stack_heads(k * jnp.exp(rf - cf), kmasks),
                                    _stack_heads(k * jnp.exp(rb - cb), kmasks)], axis=1)
            s_fb.append(_mm_nt(q_fb, k_fb))
            q2 = jnp.concatenate([q * jnp.exp(cf), q * jnp.exp(cb)], axis=1).astype(BF16)
            inter.append(_mm_nt(q2, st[n]))
        yield
        o = []
        for r, s_c, o_c in zip(rows, s_fb, inter):
            a = (s_c[:c] * ltri_w + s_c[c:] * utri_w).astype(BF16)
            o.append(_mm(a, _stack_heads(v_ref[0, r, :], vmasks)) + o_c)
        yield
        ms = [_mm((o_c * o_c).astype(BF16), avg) for o_c in o]
        yield
        for r, o_c, ms_c in zip(rows, o, ms):
            y = o_c * lax.rsqrt(ms_c + EPS) * nw
            o_ref[0, r, :] = (_silu(g_ref[0, r, :]) * y).astype(BF16)
        yield

    return p0, scan, p2


def _mix_kernel(logit_ref, rq, rk, rv, rg, gq, gk, gv, lr, gg, wgf, wgb, bg, nw, ro, go,
                r_ut, r_st, cf_s, cb_s, g_ut, g_dec, g_st):
    mixers = [_ret_phases(logit_ref, rq, rk, rv, rg, ro, r_ut, r_st),
              _gla_phases(gq, gk, gv, lr, gg, wgf, wgb, bg, nw, go, cf_s, cb_s, g_ut, g_dec, g_st)]
    n_groups = rq.shape[1] // (RET_C * CHUNK_GROUP)

    def lockstep(phase, i):
        done = object()
        gens = [m[phase](i) for m in mixers]
        while gens:
            gens = [g for g in gens if next(g, done) is not done]

    def p0(i, carry):
        lockstep(0, i)
        return carry

    def p2(i, carry):
        lockstep(2, i)
        return carry

    lax.fori_loop(0, n_groups, p0, 0)
    for m in mixers:
        m[1]()
    lax.fori_loop(0, n_groups, p2, 0)


def _mix(layer, rq, rk, rv, rg, gq, gk, gv, lr, gg, ret_decay_logit, wgf, wgb, bg, nw):
    batch, t_all, _ = rq.shape
    assert RET_C == GLA_C and t_all % (RET_C * CHUNK_GROUP) == 0
    n_chunks = t_all // RET_C
    nd = NH * GLA_DV
    tok = lambda n: pl.BlockSpec((1, t_all, n), lambda b, lg: (b, 0, 0))
    par = lambda r, n: pl.BlockSpec((None, r, n), lambda b, lg: (layer, 0, 0))
    state = lambda dt: pltpu.VMEM((n_chunks, nd, 2 * LANE), dt)
    return pl.pallas_call(
        _mix_kernel,
        grid_spec=pltpu.PrefetchScalarGridSpec(
            num_scalar_prefetch=1,
            grid=(batch,),
            in_specs=[tok(128), tok(128), tok(256), tok(256),
                      tok(128), tok(128), tok(256), tok(2 * GLA_RANK), tok(256),
                      par(2 * GLA_RANK, 128), par(2 * GLA_RANK, 128), par(2, 128), par(1, 256)],
            out_specs=[tok(256), tok(256)],
            scratch_shapes=[
                state(F32), state(BF16),
                pltpu.VMEM((t_all, LANE), F32), pltpu.VMEM((t_all, LANE), F32),
                state(F32), pltpu.VMEM((n_chunks, 8, 2 * LANE), F32), state(BF16),
            ],
        ),
        out_shape=[jax.ShapeDtypeStruct((batch, t_all, 256), BF16)] * 2,
        compiler_params=pltpu.CompilerParams(
            dimension_semantics=("arbitrary",), vmem_limit_bytes=VMEM_LIMIT),
        name="mix",
    )(ret_decay_logit[layer], rq, rk, rv, rg, gq, gk, gv, lr, gg, wgf, wgb, bg, nw)


def _attn_kernel(q_ref, k_ref, v_ref, lam_ref, nw_ref, *refs, lam_init):
    n_cast = (len(refs) - 2) // 2
    w_f32, (o_ref, *w_bf16), vx_ref = refs[:n_cast], refs[n_cast:2 * n_cast + 1], refs[-1]
    for src, dst in zip(w_f32, w_bf16):
        dst[...] = src[0].astype(BF16)
    t_all = k_ref.shape[1]
    vx_ref[:, 0:DIFF_DV] = v_ref[0]
    vx_ref[:, DIFF_DV:] = jnp.ones((t_all, DIFF_DV), BF16)
    lp = lam_ref[...]
    lam = (jnp.exp(jnp.sum(lp[0:1] * lp[1:2], axis=-1, keepdims=True))
           - jnp.exp(jnp.sum(lp[2:3] * lp[3:4], axis=-1, keepdims=True)) + lam_init)
    first = _iota((TQ, LANE), 1) < DIFF_HD

    assert CTX == TQ
    tiles = [(r, CTX if r == 0 else t_all) for r in range(0, t_all, TQ)]
    s = []
    for r, n_keys in tiles:
        q = q_ref[0, r:r + TQ, :]
        for qh in (jnp.where(first, q, jnp.zeros_like(q)), jnp.where(first, jnp.zeros_like(q), q)):
            s.append(_mm_nt(qh, k_ref[0, 0:n_keys, :]).astype(BF16))
    e = [jnp.exp2(s_h - jnp.max(s_h, axis=-1, keepdims=True)) for s_h in s]
    ol = [_mm(e_h, vx_ref[0:e_h.shape[1], :]) for e_h in e]
    for j, (r, _) in enumerate(tiles):
        o1, o2 = (x[:, 0:DIFF_DV] / x[:, DIFF_DV:] for x in ol[2 * j:2 * j + 2])
        o = o1 - lam * o2
        y = o * lax.rsqrt(jnp.mean(o * o, axis=-1, keepdims=True) + EPS) * nw_ref[...]
        o_ref[0, r:r + TQ, :] = (y * (1.0 - lam_init)).astype(BF16)


def _attn(layer, dq, dk, dv, diff_lambda, diff_norm_w, to_cast):
    batch, t_all, _ = dq.shape
    weights_f32, w_layers = zip(*to_cast)
    lam_init = 0.8 - 0.6 * math.exp(-0.3 * layer)
    head = pl.BlockSpec((1, t_all, LANE), lambda b, h: (b, 0, h))

    def slab(w):
        n_steps = max(n for n in range(1, batch * NH + 1) if w.shape[1] % (16 * n) == 0)
        return w.shape[1] // n_steps, lambda b, h: jnp.minimum(b * NH + h, n_steps - 1)

    slabs = [slab(w) for w in weights_f32]
    out = pl.pallas_call(
        functools.partial(_attn_kernel, lam_init=lam_init),
        grid=(batch, NH),
        in_specs=[
            head, head, head,
            pl.BlockSpec((None, 4, DIFF_HD), lambda b, h: (layer, 0, 0)),
            pl.BlockSpec((None, 1, DIFF_DV), lambda b, h: (layer, 0, 0)),
        ] + [pl.BlockSpec((1, r, w.shape[-1]), lambda b, h, i=i, wl=wl: (wl, i(b, h), 0))
             for w, wl, (r, i) in zip(weights_f32, w_layers, slabs)],
        out_specs=[head] + [pl.BlockSpec((r, w.shape[-1]), lambda b, h, i=i: (i(b, h), 0))
                            for w, (r, i) in zip(weights_f32, slabs)],
        out_shape=[jax.ShapeDtypeStruct((batch, t_all, NH * DIFF_DV), BF16)]
        + [jax.ShapeDtypeStruct(w.shape[1:], BF16) for w in weights_f32],
        scratch_shapes=[pltpu.VMEM((t_all, 2 * DIFF_DV), BF16)],
        compiler_params=pltpu.CompilerParams(
            dimension_semantics=("arbitrary", "arbitrary"), vmem_limit_bytes=VMEM_LIMIT),
        name="attn",
    )(dq, dk, dv, diff_lambda, diff_norm_w, *weights_f32)
    return out[0], out[1:]


def _post_kernel(*refs, n_stream, final, til):
    n = til.n_sub
    x_refs = refs[:n_stream]
    ro_refs, go_refs, do_refs = (refs[n_stream + i * n:n_stream + (i + 1) * n] for i in range(3))
    (g1_l, g1_c, sh_l, sh_c, sc_l, sc_c, g2_l, g2_c,
     wo_ref, nw_ref, wi_ref, wf_ref, fw_ref, o_ref) = refs[n_stream + 3 * n:]

    def rms(v, w_ref):
        return v * lax.rsqrt(jnp.mean(v * v, axis=-1, keepdims=True) + EPS) * w_ref[...]

    subs = range(n)
    att = [_mm(jnp.concatenate([ro_refs[j][0], go_refs[j][0], do_refs[j][0]], axis=-1), wo_ref[0]) for j in subs]
    xm = [til.load(x_refs, j) + til.mod(g1_l, g1_c, j) * att[j] for j in subs]
    h = [(rms(xm[j], nw_ref) * (1.0 + til.mod(sc_l, sc_c, j)) + til.mod(sh_l, sh_c, j)).astype(BF16) for j in subs]
    gu = [_mm(h[j], wi_ref[0]) for j in subs]
    ff = [_mm((_silu(g[:, :D_FF]) * g[:, D_FF:]).astype(BF16), wf_ref[0]) for g in gu]
    for j, (r0, r1) in enumerate(til.spans):
        xo = xm[j] + til.mod(g2_l, g2_c, j) * ff[j]
        o_ref[0, r0:r1, :] = rms(xo, fw_ref) if final else xo


def _post(layer, batch, t_all, stream, ro, go, do, mods, w_out, norm2_w, w_ffn_in, w_ffn_out, final_w, til, final):
    n_rows = t_all - til.s0 * til.sub
    wspec = lambda r, n: pl.BlockSpec((1, r, n), lambda b, t: (0, 0, 0), pipeline_mode=pl.Buffered(1))
    stream_specs, stream_ops = til.in_specs(stream, D)
    tok_specs, tok_ops = zip(*(til.in_specs(a, a.shape[-1]) for a in (ro, go, do)))
    mod_specs = sum((_mod_specs(layer, j, batch) for j in (2, 3, 4, 5)), [])
    return pl.pallas_call(
        functools.partial(_post_kernel, n_stream=len(stream_ops), final=final, til=til),
        grid=(batch, n_rows // til.rows),
        in_specs=stream_specs + sum(tok_specs, []) + mod_specs + [
            wspec(D, D),
            pl.BlockSpec((None, 1, D), lambda b, t: (layer, 0, 0)),
            wspec(D, 2 * D_FF), wspec(D_FF, D),
            pl.BlockSpec((1, D), lambda b, t: (0, 0)),
        ],
        out_specs=pl.BlockSpec((1, til.rows, D), lambda b, t: (b, t, 0)),
        out_shape=jax.ShapeDtypeStruct((batch, n_rows, D), F32),
        compiler_params=pltpu.CompilerParams(
            dimension_semantics=("arbitrary", "arbitrary"), vmem_limit_bytes=VMEM_LIMIT),
        name="post",
    )(*stream_ops, *sum(tok_ops, []), *([mods] * 8), w_out, norm2_w, w_ffn_in, w_ffn_out, final_w)


def _rope_tables(n_lat):
    lane = np.arange(LANE)
    first = ((lane % 32) < 16)[None, :]

    def pack(ang):
        cos, sin = np.cos(ang), np.sin(ang)
        tabs = (cos, np.where(first, -sin, 0.0), np.where(first, 0.0, sin))
        ident = (np.ones((CTX, LANE)), np.zeros((CTX, LANE)), np.zeros((CTX, LANE)))
        return tuple(jnp.asarray(np.concatenate([i, t], axis=0), dtype=F32) for i, t in zip(ident, tabs))

    idx = np.arange(n_lat, dtype=np.float64)
    ret_freq = 1.0 / (ROPE_BASE ** np.linspace(0.0, 1.0, RET_DK // 2))
    ret = pack((idx[:, None] * ret_freq[None, :])[:, lane % 16])
    ax_freq = 1.0 / (ROPE_BASE ** (np.arange(DIFF_HD // 4) / (DIFF_HD // 4)))
    row_ang = np.floor(idx / GRID_W)[:, None] * ax_freq[None, :]
    col_ang = (idx % GRID_W)[:, None] * ax_freq[None, :]
    is_row = ((lane % 64) < 32)[None, :]
    diff = pack(np.where(is_row, row_ang[:, lane % 16], col_ang[:, lane % 16]))
    return ret + diff


def kernel(x, c, ctx, c_ctx, w_ada, b_ada, norm1_w, w_in, ret_decay_logit, gla_w_gate, gla_b_gate,
           gla_norm_w, diff_lambda, diff_norm_w, w_out, norm2_w, w_ffn_in, w_ffn_out, final_norm_w):
    batch, n_lat, _ = x.shape
    assert ctx.shape[1] == CTX and batch < 16

    zpad = jnp.zeros((DEPTH, GLA_RANK, NH * GLA_DK), F32)
    wgf = jnp.concatenate([gla_w_gate[:, 0], zpad], axis=1).astype(BF16)
    wgb = jnp.concatenate([zpad, gla_w_gate[:, 1]], axis=1).astype(BF16)
    gla_nw = jnp.tile(gla_norm_w, (1, NH)).reshape(DEPTH, 1, NH * GLA_DV)
    norm1 = norm1_w.reshape(DEPTH, 1, D)
    norm2 = norm2_w.reshape(DEPTH, 1, D)
    diff_nw = diff_norm_w.reshape(DEPTH, 1, DIFF_DV)
    final_w = final_norm_w.reshape(1, D)
    tabs = _rope_tables(n_lat)

    cond = jnp.concatenate([c, c_ctx[None, :], jnp.zeros((16 - batch - 1, D), F32)], axis=0)
    mods, w_in_b = _ada(cond, w_ada, b_ada, w_in)
    mods = mods.reshape(DEPTH * 16 * 6, 1, D)

    t_all = CTX + n_lat
    stream = (ctx, x)
    for layer in range(DEPTH):
        final = layer == DEPTH - 1
        til = _Tiling(*POST_TILING["first" if layer == 0 else "last" if final else "middle"])
        rq, rk, rv, rg, gq, gk, gv, gr, dq, dk, dv, lr = _proj(
            layer, batch, t_all, stream, mods, norm1, w_in_b[None], tabs)
        ro, go = _mix(layer, rq, rk, rv, rg, gq, gk, gv, lr, gr, ret_decay_logit, wgf, wgb, gla_b_gate, gla_nw)
        to_cast = [(w_out, layer), (w_ffn_in, layer), (w_ffn_out, layer)] + ([] if final else [(w_in, layer + 1)])
        do, (wo_b, wi_b, wf_b, *w_in_next) = _attn(layer, dq, dk, dv, diff_lambda, diff_nw, to_cast)
        stream = _post(layer, batch, t_all, stream, ro, go, do, mods, wo_b[None], norm2, wi_b[None], wf_b[None],
                       final_w, til, final)
        if w_in_next:
            w_in_b = w_in_next[0]
    return stream
```

```python
import functools
import math

import jax
import jax.numpy as jnp
import numpy as np
from jax import lax
from jax.experimental import pallas as pl
from jax.experimental.pallas import tpu as pltpu

F32 = jnp.float32
BF16 = jnp.bfloat16

D = 1024
DEPTH = 4
CTX = 256
GRID_W = 64
EPS = 1e-6
GN_EPS = 1e-5
ROPE_BASE = 10000.0

NH = 4
RET_DK, RET_DV = 32, 64
GLA_DK, GLA_DV = 32, 64
GLA_RANK = 16
GLA_TAU = 16.0
DIFF_HD, DIFF_DV = 64, 128
D_FF = 2816

O_RQ, O_RK, O_RV, O_RG = 0, 128, 256, 512
O_GQ, O_GK, O_GV, O_GR = 768, 896, 1024, 1280
O_LR = 1536
N_PROJ_A = O_LR + 2 * GLA_RANK

LANE = 128
PROJ_SUBTILES = 3
POST_TILING = {"first": (256, 3, 0), "middle": (288, 2, 0), "last": (256, 2, 1)}
TQ = 256
N_CAST = 3
RET_C = 128
GLA_C = 128
CHUNK_GROUP = 9
VMEM_LIMIT = 56 * 1024 * 1024


def _mm(a, b):
    return jnp.dot(a, b, preferred_element_type=F32)


def _mm_nt(a, b):
    return lax.dot_general(a, b, (((1,), (1,)), ((), ())), preferred_element_type=F32)


def _mm_tn(a, b):
    return lax.dot_general(a, b, (((0,), (0,)), ((), ())), preferred_element_type=F32)


def _split(x):
    hi = x.astype(BF16)
    lo = (x - hi.astype(F32)).astype(BF16)
    return hi, lo


def _hilo_mm(m_bf16, x):
    hi, lo = _split(x)
    n = x.shape[1]
    r = _mm(m_bf16, jnp.concatenate([hi, lo], axis=1))
    return r[:, :n] + r[:, n:]


def _group_mean(x, avg_bf16):
    hi, lo = _split(x)
    m = x.shape[0]
    r = _mm(jnp.concatenate([hi, lo], axis=0), avg_bf16)
    return r[:m] + r[m:]


def _log_sigmoid(z):
    return jnp.minimum(z, 0.0) - jnp.log(1.0 + jnp.exp(-jnp.abs(z)))


def _silu(z):
    return z / (1.0 + jnp.exp(-z))


def _iota(shape, dim):
    return lax.broadcasted_iota(jnp.int32, shape, dim)


def _head_masks(rows, lanes, head_shift):
    head = _iota((rows, lanes), 1) >> head_shift
    return [head == h for h in range(NH)]


def _stack_heads(x, masks):
    x = x.astype(F32)
    return jnp.concatenate([jnp.where(m, x, 0.0).astype(BF16) for m in masks], axis=0)


def _group_avg_matrix(n, group_shift):
    r = _iota((n, n), 0) >> group_shift
    c = _iota((n, n), 1) >> group_shift
    return jnp.where(r == c, 1.0 / (1 << group_shift), 0.0).astype(BF16)


def _ada_kernel(cond_ref, w_ref, b_ref, o_ref):
    cond = _silu(cond_ref[...])
    w = w_ref[0]
    c_hi, c_lo = _split(cond)
    w_hi, w_lo = _split(w)
    hh = _mm(jnp.concatenate([c_hi, c_lo], axis=0), w_hi)
    o_ref[0] = hh[:16] + hh[16:] + _mm(c_hi, w_lo) + b_ref[0]


def _ada(cond, w_ada, b_ada):
    bn = 1536
    n = w_ada.shape[-1]
    return pl.pallas_call(
        _ada_kernel,
        grid=(DEPTH, n // bn),
        in_specs=[
            pl.BlockSpec((16, D), lambda l, j: (0, 0)),
            pl.BlockSpec((1, D, bn), lambda l, j: (l, 0, j)),
            pl.BlockSpec((1, 1, bn), lambda l, j: (l, 0, j)),
        ],
        out_specs=pl.BlockSpec((1, 16, bn), lambda l, j: (l, 0, j)),
        out_shape=jax.ShapeDtypeStruct((DEPTH, 16, n), F32),
        compiler_params=pltpu.CompilerParams(dimension_semantics=("arbitrary", "arbitrary")),
        name="ada",
    )(cond, w_ada, b_ada.reshape(DEPTH, 1, n))


def _mod_specs(layer, j, batch):
    lat = pl.BlockSpec((None, 1, D), lambda b, t: ((layer * 16 + b) * 6 + j, 0, 0))
    ctx = pl.BlockSpec((None, 1, D), lambda b, t: ((layer * 16 + batch) * 6 + j, 0, 0))
    return [lat, ctx]


class _Tiling:
    def __init__(self, sub, n_sub, s0=0):
        self.sub, self.n_sub, self.s0 = sub, n_sub, s0
        self.rows = sub * n_sub
        self.spans = [(j * sub, (j + 1) * sub) for j in range(n_sub)]

    def s(self, j, t=None):
        return (pl.program_id(1) if t is None else t) * self.n_sub + j + self.s0

    def in_specs(self, arr, width):
        specs, ops = [], []
        for j in range(self.n_sub):
            if isinstance(arr, tuple):
                assert CTX % self.sub == 0
                n_ctx = CTX // self.sub
                specs += [
                    pl.BlockSpec((1, self.sub, width), lambda b, t, j=j: (b, jnp.minimum(self.s(j, t), n_ctx - 1), 0)),
                    pl.BlockSpec((1, self.sub, width), lambda b, t, j=j: (b, jnp.maximum(self.s(j, t) - n_ctx, 0), 0))]
                ops += list(arr)
            else:
                specs.append(pl.BlockSpec((1, self.sub, width), lambda b, t, j=j: (b, self.s(j, t), 0)))
                ops.append(arr)
        return specs, ops

    def load(self, refs, j):
        if len(refs) == 2 * self.n_sub:
            return jnp.where(self.s(j) < CTX // self.sub, refs[2 * j][0], refs[2 * j + 1][0])
        return refs[j][0]

    def mod(self, lat_ref, ctx_ref, j):
        if CTX % self.sub == 0:
            return jnp.where(self.s(j) < CTX // self.sub, ctx_ref[...], lat_ref[...])
        assert self.sub > CTX and self.s0 == 0
        if j > 0:
            return lat_ref[...]
        ctx_rows = (_iota((self.sub, 1), 0) < CTX) & (pl.program_id(1) == 0)
        return jnp.where(ctx_rows, ctx_ref[...], lat_ref[...])


def _rope(x, cos, sa, sb):
    outs = []
    for j in range(x.shape[1] // LANE):
        blk = x[:, j * LANE:(j + 1) * LANE]
        fwd = pltpu.roll(blk, LANE - 16, 1)
        bwd = pltpu.roll(blk, 16, 1)
        outs.append(blk * cos + fwd * sa + bwd * sb)
    return outs[0] if len(outs) == 1 else jnp.concatenate(outs, axis=1)


def _proj_kernel(*refs, n_stream, til):
    (sh_l, sh_c, sc_l, sc_c, nw_ref, w_ref, rcos, rsa, rsb, dcos, dsa, dsb,
     rq_o, rk_o, rv_o, rg_o, gq_o, gk_o, gv_o, gr_o, dq_o, dk_o, dv_o, lr_o, wa_ref, wd_ref) = refs[n_stream:]

    @pl.when((pl.program_id(0) == 0) & (pl.program_id(1) == 0))
    def _():
        wa_ref[...] = w_ref[0, :, 0:N_PROJ_A]
        wd_ref[...] = w_ref[0, :, N_PROJ_A:]

    wa, wd = wa_ref[...], wd_ref[...]
    h = []
    for j in range(til.n_sub):
        xj = til.load(refs[:n_stream], j)
        y = xj * lax.rsqrt(jnp.mean(xj * xj, axis=-1, keepdims=True) + EPS) * nw_ref[...]
        h.append((y * (1.0 + til.mod(sc_l, sc_c, j)) + til.mod(sh_l, sh_c, j)).astype(BF16))
    p = [(_mm(hj, wa), _mm(hj, wd)) for hj in h]

    avg = _group_avg_matrix(NH * RET_DV, 6)
    for (r0, r1), (pa, pd) in zip(til.spans, p):
        rc, ra, rb = rcos[r0:r1, :], rsa[r0:r1, :], rsb[r0:r1, :]
        rq_o[0, r0:r1, :] = _rope(pa[:, O_RQ:O_RQ + 128], rc, ra, rb).astype(BF16)
        rk_o[0, r0:r1, :] = (_rope(pa[:, O_RK:O_RK + 128], rc, ra, rb) * (RET_DK ** -0.5)).astype(BF16)
        rv = pa[:, O_RV:O_RV + 256]
        rv_o[0, r0:r1, :] = (rv - _group_mean(rv, avg)).astype(BF16)
        rg_o[0, r0:r1, :] = pa[:, O_RG:O_RG + 256]
        gq_o[0, r0:r1, :] = (pa[:, O_GQ:O_GQ + 128] * (GLA_DK ** -0.5)).astype(BF16)
        gk_o[0, r0:r1, :] = pa[:, O_GK:O_GK + 128].astype(BF16)
        gv_o[0, r0:r1, :] = pa[:, O_GV:O_GV + 256].astype(BF16)
        gr_o[0, r0:r1, :] = pa[:, O_GR:O_GR + 256]
        lr_o[0, r0:r1, :] = pa[:, O_LR:O_LR + 2 * GLA_RANK].astype(BF16)
        dc, da, db = dcos[r0:r1, :], dsa[r0:r1, :], dsb[r0:r1, :]
        dq_o[0, r0:r1, :] = (_rope(pd[:, 0:512], dc, da, db)
                             * (DIFF_HD ** -0.5 * math.log2(math.e))).astype(BF16)
        dk_o[0, r0:r1, :] = _rope(pd[:, 512:1024], dc, da, db).astype(BF16)
        dv_o[0, r0:r1, :] = pd[:, 1024:1536].astype(BF16)


def _proj(layer, batch, t_all, stream, mods, norm1_w, w_in_b, tabs):
    til = _Tiling(CTX, PROJ_SUBTILES)
    tm = til.rows
    tok = lambda n: pl.BlockSpec((1, tm, n), lambda b, t: (b, t, 0))
    tab = pl.BlockSpec((tm, LANE), lambda b, t: (t, 0))
    widths = [128, 128, 256, 256, 128, 128, 256, 256, 512, 512, 512, 2 * GLA_RANK]
    dtypes = [BF16, BF16, BF16, F32, BF16, BF16, BF16, F32, BF16, BF16, BF16, BF16]
    stream_specs, stream_ops = til.in_specs(stream, D)
    n_proj = w_in_b.shape[-1]
    return pl.pallas_call(
        functools.partial(_proj_kernel, n_stream=len(stream_ops), til=til),
        grid=(batch, t_all // tm),
        in_specs=stream_specs + _mod_specs(layer, 0, batch) + _mod_specs(layer, 1, batch) + [
            pl.BlockSpec((None, 1, D), lambda b, t: (layer, 0, 0)),
            pl.BlockSpec((1, D, n_proj), lambda b, t: (layer, 0, 0), pipeline_mode=pl.Buffered(1)),
            tab, tab, tab, tab, tab, tab,
        ],
        out_specs=[tok(n) for n in widths],
        out_shape=[jax.ShapeDtypeStruct((batch, t_all, n), dt) for n, dt in zip(widths, dtypes)],
        scratch_shapes=[pltpu.VMEM((D, N_PROJ_A), BF16), pltpu.VMEM((D, n_proj - N_PROJ_A), BF16)],
        compiler_params=pltpu.CompilerParams(
            dimension_semantics=("arbitrary", "arbitrary"), vmem_limit_bytes=VMEM_LIMIT),
        name="proj",
    )(*stream_ops, mods, mods, mods, mods, norm1_w, w_in_b, *tabs)


def _ret_phases(logit_ref, q_ref, k_ref, v_ref, g_ref, o_ref, ut, st):
    n_chunks = q_ref.shape[1] // RET_C
    n_ctx = CTX // RET_C
    c = RET_C

    lane_head = _iota((c, LANE), 1) >> 5
    i_f = _iota((c, LANE), 0).astype(F32)

    def lanes_of(d):
        z = jnp.zeros((c, LANE), F32)
        for h in range(NH):
            z = jnp.where(lane_head == h, logit_ref[d, h], z)
        return _log_sigmoid(z)

    lg_f, lg_b = lanes_of(0), lanes_of(1)
    kdec_f = jnp.exp((c - 1.0 - i_f) * lg_f)
    kdec_b = jnp.exp(i_f * lg_b)
    qdec_f = jnp.exp((i_f + 1.0) * lg_f)
    qdec_b = jnp.exp((c - i_f) * lg_b)
    cd_f = jnp.exp(c * lg_f[0:1])
    cd_b = jnp.exp(c * lg_b[0:1])
    kdec = jnp.concatenate([kdec_f, kdec_b], axis=1)
    qdec = jnp.concatenate([qdec_f, qdec_b], axis=1)

    nd = NH * RET_DV
    bd = (_iota((nd, 2 * LANE), 0) >> 6) == ((_iota((nd, 2 * LANE), 1) & (LANE - 1)) >> 5)

    def p0(i):
        for j in range(CHUNK_GROUP):
            n = i * CHUNK_GROUP + j
            rows = pl.ds(pl.multiple_of(n * c, c), c)
            k = k_ref[0, rows, :].astype(F32)
            k2 = (jnp.concatenate([k, k], axis=1) * kdec).astype(BF16)
            ut[n] = jnp.where(bd, _mm_tn(v_ref[0, rows, :], k2), 0.0)
        yield

    def scan():
        s = jnp.zeros((nd, LANE), F32)
        for n in range(n_chunks):
            st[n, :, 0:LANE] = s.astype(BF16)
            s = s * cd_f + ut[n, :, 0:LANE]
        s = jnp.zeros((nd, LANE), F32)
        for n in list(range(n_ctx - 1, -1, -1)) + list(range(n_chunks - 1, n_ctx - 1, -1)):
            st[n, :, LANE:] = s.astype(BF16)
            s = s * cd_b + ut[n, :, LANE:]

    rel = (_iota((c, c), 0) - _iota((c, c), 1)).astype(F32)
    eye = jnp.where(rel == 0.0, 1.0, 0.0)
    dmats = []
    for h in range(NH):
        lf = _log_sigmoid(jnp.full((c, c), logit_ref[0, h], F32))
        lb = _log_sigmoid(jnp.full((c, c), logit_ref[1, h], F32))
        dmats.append(jnp.exp(jnp.where(rel >= 0.0, rel * lf, -rel * lb)) + eye)
    dwide = jnp.concatenate(dmats, axis=1)
    kmasks, vmasks = _head_masks(c, LANE, 5), _head_masks(c, NH * RET_DV, 6)
    avg = _group_avg_matrix(NH * RET_DV, 6)

    def p2(i):
        ns = [i * CHUNK_GROUP + j for j in range(CHUNK_GROUP)]
        rows = [pl.ds(pl.multiple_of(n * c, c), c) for n in ns]
        sc, inter = [], []
        for n, r in zip(ns, rows):
            q = q_ref[0, r, :]
            sc.append(_mm_nt(q, _stack_heads(k_ref[0, r, :], kmasks)))
            qf = q.astype(F32)
            inter.append(_mm_nt((jnp.concatenate([qf, qf], axis=1) * qdec).astype(BF16), st[n]))
        yield
        o = []
        for r, s_c, o_c in zip(rows, sc, inter):
            o.append(_mm((s_c * dwide).astype(BF16), _stack_heads(v_ref[0, r, :], vmasks)) + o_c)
        yield
        var = [_mm((o_c * o_c).astype(BF16), avg) for o_c in o]
        yield
        for r, o_c, v_c in zip(rows, o, var):
            o_ref[0, r, :] = (_silu(g_ref[0, r, :]) * o_c * lax.rsqrt(v_c + GN_EPS)).astype(BF16)
        yield

    return p0, scan, p2


def _gla_phases(q_ref, k_ref, v_ref, lr_ref, g_ref, wgf_ref, wgb_ref, bg_ref, nw_ref, o_ref,
                cf_s, cb_s, ut, dec, st):
    c = GLA_C
    n_chunks = q_ref.shape[1] // c
    n_ctx = CTX // c
    nd = NH * GLA_DV

    ltri = jnp.where(_iota((c, c), 0) >= _iota((c, c), 1), 1.0, 0.0)
    utri = jnp.where(_iota((c, c), 0) <= _iota((c, c), 1), 1.0, 0.0)
    ltri_b, utri_b = ltri.astype(BF16), utri.astype(BF16)
    bd = (_iota((nd, 2 * LANE), 0) >> 6) == ((_iota((nd, 2 * LANE), 1) & (LANE - 1)) >> 5)

    lr = lr_ref[0]
    cf_s[...] = _log_sigmoid(_mm(lr, wgf_ref[...]) + bg_ref[0:1, :]) * (1.0 / GLA_TAU)
    cb_s[...] = _log_sigmoid(_mm(lr, wgb_ref[...]) + bg_ref[1:2, :]) * (1.0 / GLA_TAU)

    def p0(i):
        ns = [i * CHUNK_GROUP + j for j in range(CHUNK_GROUP)]
        rows = [pl.ds(pl.multiple_of(n * c, c), c) for n in ns]
        cf = [_hilo_mm(ltri_b, cf_s[r, :]) for r in rows]
        cb = [_hilo_mm(utri_b, cb_s[r, :]) for r in rows]
        yield
        for n, r, cf_c, cb_c in zip(ns, rows, cf, cb):
            cf_s[r, :] = cf_c
            cb_s[r, :] = cb_c
            cf_end, cb_end = cf_c[c - 1:c, :], cb_c[0:1, :]
            k = k_ref[0, r, :].astype(F32)
            k2 = jnp.concatenate([k * jnp.exp(cf_end - cf_c), k * jnp.exp(cb_end - cb_c)], axis=1).astype(BF16)
            ut[n] = jnp.where(bd, _mm_tn(v_ref[0, r, :], k2), 0.0)
            dec[n] = jnp.broadcast_to(jnp.exp(jnp.concatenate([cf_end, cb_end], axis=1)), (8, 2 * LANE))
        yield

    def scan():
        def scan_f(n, s):
            st[n, :, 0:LANE] = s.astype(BF16)
            return s * dec[n, 0:1, 0:LANE] + ut[n, :, 0:LANE]

        def scan_b(i, s, hi):
            n = hi - 1 - i
            st[n, :, LANE:] = s.astype(BF16)
            return s * dec[n, 0:1, LANE:] + ut[n, :, LANE:]

        zero = jnp.zeros((nd, LANE), F32)
        lax.fori_loop(0, n_chunks, scan_f, zero)
        s_ctx = lax.fori_loop(0, n_ctx, functools.partial(scan_b, hi=n_ctx), zero)
        lax.fori_loop(0, n_chunks - n_ctx, functools.partial(scan_b, hi=n_chunks), s_ctx)

    avg = _group_avg_matrix(nd, 6)
    nw = nw_ref[...]
    ltri_w = jnp.concatenate([ltri] * NH, axis=1)
    utri_w = jnp.concatenate([utri] * NH, axis=1)

    kmasks, vmasks = _head_masks(c, LANE, 5), _head_masks(c, nd, 6)

    def p2(i):
        ns = [i * CHUNK_GROUP + j for j in range(CHUNK_GROUP)]
        rows = [pl.ds(pl.multiple_of(n * c, c), c) for n in ns]
        s_fb, inter = [], []
        for n, r in zip(ns, rows):
            q = q_ref[0, r, :].astype(F32)
            k = k_ref[0, r, :].astype(F32)
            cf, cb = cf_s[r, :], cb_s[r, :]
            rf, rb = cf[c // 2 - 1:c // 2, :], cb[c // 2:c // 2 + 1, :]
            qf, qb = (q * jnp.exp(cf - rf)).astype(BF16), (q * jnp.exp(cb - rb)).astype(BF16)
            zq = jnp.zeros_like(qf)
            q_fb = jnp.concatenate([jnp.concatenate([qf, zq], axis=1), jnp.concatenate([zq, qb], axis=1)], axis=0)
            k_fb = jnp.concatenate([_stack_heads(k * jnp.exp(rf - cf), kmasks),
                                    _stack_heads(k * jnp.exp(rb - cb), kmasks)], axis=1)
            s_fb.append(_mm_nt(q_fb, k_fb))
            q2 = jnp.concatenate([q * jnp.exp(cf), q * jnp.exp(cb)], axis=1).astype(BF16)
            inter.append(_mm_nt(q2, st[n]))
        yield
        o = []
        for r, s_c, o_c in zip(rows, s_fb, inter):
            a = (s_c[:c] * ltri_w + s_c[c:] * utri_w).astype(BF16)
            o.append(_mm(a, _stack_heads(v_ref[0, r, :], vmasks)) + o_c)
        yield
        ms = [_mm((o_c * o_c).astype(BF16), avg) for o_c in o]
        yield
        for r, o_c, ms_c in zip(rows, o, ms):
            y = o_c * lax.rsqrt(ms_c + EPS) * nw
            o_ref[0, r, :] = (_silu(g_ref[0, r, :]) * y).astype(BF16)
        yield

    return p0, scan, p2


def _mix_kernel(logit_ref, rq, rk, rv, rg, gq, gk, gv, lr, gg, wgf, wgb, bg, nw, ro, go,
                r_ut, r_st, cf_s, cb_s, g_ut, g_dec, g_st):
    mixers = [_ret_phases(logit_ref, rq, rk, rv, rg, ro, r_ut, r_st),
              _gla_phases(gq, gk, gv, lr, gg, wgf, wgb, bg, nw, go, cf_s, cb_s, g_ut, g_dec, g_st)]
    n_groups = rq.shape[1] // (RET_C * CHUNK_GROUP)

    def lockstep(phase, i):
        done = object()
        gens = [m[phase](i) for m in mixers]
        while gens:
            gens = [g for g in gens if next(g, done) is not done]

    def p0(i, carry):
        lockstep(0, i)
        return carry

    def p2(i, carry):
        lockstep(2, i)
        return carry

    lax.fori_loop(0, n_groups, p0, 0)
    for m in mixers:
        m[1]()
    lax.fori_loop(0, n_groups, p2, 0)


def _mix(layer, rq, rk, rv, rg, gq, gk, gv, lr, gg, ret_decay_logit, wgf, wgb, bg, nw):
    batch, t_all, _ = rq.shape
    assert RET_C == GLA_C and t_all % (RET_C * CHUNK_GROUP) == 0
    n_chunks = t_all // RET_C
    nd = NH * GLA_DV
    tok = lambda n: pl.BlockSpec((1, t_all, n), lambda b, lg: (b, 0, 0))
    par = lambda r, n: pl.BlockSpec((None, r, n), lambda b, lg: (layer, 0, 0))
    state = lambda dt: pltpu.VMEM((n_chunks, nd, 2 * LANE), dt)
    return pl.pallas_call(
        _mix_kernel,
        grid_spec=pltpu.PrefetchScalarGridSpec(
            num_scalar_prefetch=1,
            grid=(batch,),
            in_specs=[tok(128), tok(128), tok(256), tok(256),
                      tok(128), tok(128), tok(256), tok(2 * GLA_RANK), tok(256),
                      par(2 * GLA_RANK, 128), par(2 * GLA_RANK, 128), par(2, 128), par(1, 256)],
            out_specs=[tok(256), tok(256)],
            scratch_shapes=[
                state(F32), state(BF16),
                pltpu.VMEM((t_all, LANE), F32), pltpu.VMEM((t_all, LANE), F32),
                state(F32), pltpu.VMEM((n_chunks, 8, 2 * LANE), F32), state(BF16),
            ],
        ),
        out_shape=[jax.ShapeDtypeStruct((batch, t_all, 256), BF16)] * 2,
        compiler_params=pltpu.CompilerParams(
            dimension_semantics=("arbitrary",), vmem_limit_bytes=VMEM_LIMIT),
        name="mix",
    )(ret_decay_logit[layer], rq, rk, rv, rg, gq, gk, gv, lr, gg, wgf, wgb, bg, nw)


def _attn_kernel(q_ref, k_ref, v_ref, lam_ref, nw_ref, *refs, lam_init):
    w_f32, (o_ref, *w_bf16), vx_ref = refs[:N_CAST], refs[N_CAST:2 * N_CAST + 1], refs[-1]
    for src, dst in zip(w_f32, w_bf16):
        dst[...] = src[0].astype(BF16)
    t_all = k_ref.shape[1]
    vx_ref[:, 0:DIFF_DV] = v_ref[0]
    vx_ref[:, DIFF_DV:] = jnp.ones((t_all, DIFF_DV), BF16)
    lp = lam_ref[...]
    lam = (jnp.exp(jnp.sum(lp[0:1] * lp[1:2], axis=-1, keepdims=True))
           - jnp.exp(jnp.sum(lp[2:3] * lp[3:4], axis=-1, keepdims=True)) + lam_init)
    first = _iota((TQ, LANE), 1) < DIFF_HD

    assert CTX == TQ
    tiles = [(r, CTX if r == 0 else t_all) for r in range(0, t_all, TQ)]
    s = []
    for r, n_keys in tiles:
        q = q_ref[0, r:r + TQ, :]
        for qh in (jnp.where(first, q, jnp.zeros_like(q)), jnp.where(first, jnp.zeros_like(q), q)):
            s.append(_mm_nt(qh, k_ref[0, 0:n_keys, :]).astype(BF16))
    e = [jnp.exp2(s_h - jnp.max(s_h, axis=-1, keepdims=True)) for s_h in s]
    ol = [_mm(e_h, vx_ref[0:e_h.shape[1], :]) for e_h in e]
    for j, (r, _) in enumerate(tiles):
        o1, o2 = (x[:, 0:DIFF_DV] / x[:, DIFF_DV:] for x in ol[2 * j:2 * j + 2])
        o = o1 - lam * o2
        y = o * lax.rsqrt(jnp.mean(o * o, axis=-1, keepdims=True) + EPS) * nw_ref[...]
        o_ref[0, r:r + TQ, :] = (y * (1.0 - lam_init)).astype(BF16)


def _attn(layer, dq, dk, dv, diff_lambda, diff_norm_w, weights_f32):
    batch, t_all, _ = dq.shape
    assert len(weights_f32) == N_CAST
    lam_init = 0.8 - 0.6 * math.exp(-0.3 * layer)
    head = pl.BlockSpec((1, t_all, LANE), lambda b, h: (b, 0, h))

    def slab(w):
        n_steps = max(n for n in range(1, batch * NH + 1) if w.shape[1] % (16 * n) == 0)
        return w.shape[1] // n_steps, lambda b, h: jnp.minimum(b * NH + h, n_steps - 1)

    slabs = [slab(w) for w in weights_f32]
    out = pl.pallas_call(
        functools.partial(_attn_kernel, lam_init=lam_init),
        grid=(batch, NH),
        in_specs=[
            head, head, head,
            pl.BlockSpec((None, 4, DIFF_HD), lambda b, h: (layer, 0, 0)),
            pl.BlockSpec((None, 1, DIFF_DV), lambda b, h: (layer, 0, 0)),
        ] + [pl.BlockSpec((1, r, w.shape[-1]), lambda b, h, i=i: (layer, i(b, h), 0))
             for w, (r, i) in zip(weights_f32, slabs)],
        out_specs=[head] + [pl.BlockSpec((r, w.shape[-1]), lambda b, h, i=i: (i(b, h), 0))
                            for w, (r, i) in zip(weights_f32, slabs)],
        out_shape=[jax.ShapeDtypeStruct((batch, t_all, NH * DIFF_DV), BF16)]
        + [jax.ShapeDtypeStruct(w.shape[1:], BF16) for w in weights_f32],
        scratch_shapes=[pltpu.VMEM((t_all, 2 * DIFF_DV), BF16)],
        compiler_params=pltpu.CompilerParams(
            dimension_semantics=("arbitrary", "arbitrary"), vmem_limit_bytes=VMEM_LIMIT),
        name="attn",
    )(dq, dk, dv, diff_lambda, diff_norm_w, *weights_f32)
    return out[0], out[1:]


def _post_kernel(*refs, n_stream, final, til):
    n = til.n_sub
    x_refs = refs[:n_stream]
    ro_refs, go_refs, do_refs = (refs[n_stream + i * n:n_stream + (i + 1) * n] for i in range(3))
    (g1_l, g1_c, sh_l, sh_c, sc_l, sc_c, g2_l, g2_c,
     wo_ref, nw_ref, wi_ref, wf_ref, fw_ref, o_ref) = refs[n_stream + 3 * n:]

    def rms(v, w_ref):
        return v * lax.rsqrt(jnp.mean(v * v, axis=-1, keepdims=True) + EPS) * w_ref[...]

    subs = range(n)
    att = [_mm(jnp.concatenate([ro_refs[j][0], go_refs[j][0], do_refs[j][0]], axis=-1), wo_ref[0]) for j in subs]
    xm = [til.load(x_refs, j) + til.mod(g1_l, g1_c, j) * att[j] for j in subs]
    h = [(rms(xm[j], nw_ref) * (1.0 + til.mod(sc_l, sc_c, j)) + til.mod(sh_l, sh_c, j)).astype(BF16) for j in subs]
    gu = [_mm(h[j], wi_ref[0]) for j in subs]
    ff = [_mm((_silu(g[:, :D_FF]) * g[:, D_FF:]).astype(BF16), wf_ref[0]) for g in gu]
    for j, (r0, r1) in enumerate(til.spans):
        xo = xm[j] + til.mod(g2_l, g2_c, j) * ff[j]
        o_ref[0, r0:r1, :] = rms(xo, fw_ref) if final else xo


def _post(layer, batch, t_all, stream, ro, go, do, mods, w_out, norm2_w, w_ffn_in, w_ffn_out, final_w, til, final):
    n_rows = t_all - til.s0 * til.sub
    wspec = lambda r, n: pl.BlockSpec((1, r, n), lambda b, t: (0, 0, 0), pipeline_mode=pl.Buffered(1))
    stream_specs, stream_ops = til.in_specs(stream, D)
    tok_specs, tok_ops = zip(*(til.in_specs(a, a.shape[-1]) for a in (ro, go, do)))
    mod_specs = sum((_mod_specs(layer, j, batch) for j in (2, 3, 4, 5)), [])
    return pl.pallas_call(
        functools.partial(_post_kernel, n_stream=len(stream_ops), final=final, til=til),
        grid=(batch, n_rows // til.rows),
        in_specs=stream_specs + sum(tok_specs, []) + mod_specs + [
            wspec(D, D),
            pl.BlockSpec((None, 1, D), lambda b, t: (layer, 0, 0)),
            wspec(D, 2 * D_FF), wspec(D_FF, D),
            pl.BlockSpec((1, D), lambda b, t: (0, 0)),
        ],
        out_specs=pl.BlockSpec((1, til.rows, D), lambda b, t: (b, t, 0)),
        out_shape=jax.ShapeDtypeStruct((batch, n_rows, D), F32),
        compiler_params=pltpu.CompilerParams(
            dimension_semantics=("arbitrary", "arbitrary"), vmem_limit_bytes=VMEM_LIMIT),
        name="post",
    )(*stream_ops, *sum(tok_ops, []), *([mods] * 8), w_out, norm2_w, w_ffn_in, w_ffn_out, final_w)


def _rope_tables(n_lat):
    lane = np.arange(LANE)
    first = ((lane % 32) < 16)[None, :]

    def pack(ang):
        cos, sin = np.cos(ang), np.sin(ang)
        tabs = (cos, np.where(first, -sin, 0.0), np.where(first, 0.0, sin))
        ident = (np.ones((CTX, LANE)), np.zeros((CTX, LANE)), np.zeros((CTX, LANE)))
        return tuple(jnp.asarray(np.concatenate([i, t], axis=0), dtype=F32) for i, t in zip(ident, tabs))

    idx = np.arange(n_lat, dtype=np.float64)
    ret_freq = 1.0 / (ROPE_BASE ** np.linspace(0.0, 1.0, RET_DK // 2))
    ret = pack((idx[:, None] * ret_freq[None, :])[:, lane % 16])
    ax_freq = 1.0 / (ROPE_BASE ** (np.arange(DIFF_HD // 4) / (DIFF_HD // 4)))
    row_ang = np.floor(idx / GRID_W)[:, None] * ax_freq[None, :]
    col_ang = (idx % GRID_W)[:, None] * ax_freq[None, :]
    is_row = ((lane % 64) < 32)[None, :]
    diff = pack(np.where(is_row, row_ang[:, lane % 16], col_ang[:, lane % 16]))
    return ret + diff


def kernel(x, c, ctx, c_ctx, w_ada, b_ada, norm1_w, w_in, ret_decay_logit, gla_w_gate, gla_b_gate,
           gla_norm_w, diff_lambda, diff_norm_w, w_out, norm2_w, w_ffn_in, w_ffn_out, final_norm_w):
    batch, n_lat, _ = x.shape
    assert ctx.shape[1] == CTX and batch < 16

    w_in_b = w_in.astype(BF16)
    zpad = jnp.zeros((DEPTH, GLA_RANK, NH * GLA_DK), F32)
    wgf = jnp.concatenate([gla_w_gate[:, 0], zpad], axis=1).astype(BF16)
    wgb = jnp.concatenate([zpad, gla_w_gate[:, 1]], axis=1).astype(BF16)
    gla_nw = jnp.tile(gla_norm_w, (1, NH)).reshape(DEPTH, 1, NH * GLA_DV)
    norm1 = norm1_w.reshape(DEPTH, 1, D)
    norm2 = norm2_w.reshape(DEPTH, 1, D)
    diff_nw = diff_norm_w.reshape(DEPTH, 1, DIFF_DV)
    final_w = final_norm_w.reshape(1, D)
    tabs = _rope_tables(n_lat)

    cond = jnp.concatenate([c, c_ctx[None, :], jnp.zeros((16 - batch - 1, D), F32)], axis=0)
    mods = _ada(cond, w_ada, b_ada).reshape(DEPTH * 16 * 6, 1, D)

    t_all = CTX + n_lat
    stream = (ctx, x)
    for layer in range(DEPTH):
        final = layer == DEPTH - 1
        til = _Tiling(*POST_TILING["first" if layer == 0 else "last" if final else "middle"])
        rq, rk, rv, rg, gq, gk, gv, gr, dq, dk, dv, lr = _proj(
            layer, batch, t_all, stream, mods, norm1, w_in_b, tabs)
        ro, go = _mix(layer, rq, rk, rv, rg, gq, gk, gv, lr, gr, ret_decay_logit, wgf, wgb, gla_b_gate, gla_nw)
        do, (wo_b, wi_b, wf_b) = _attn(layer, dq, dk, dv, diff_lambda, diff_nw, (w_out, w_ffn_in, w_ffn_out))
        stream = _post(layer, batch, t_all, stream, ro, go, do, mods, wo_b[None], norm2, wi_b[None], wf_b[None],
                       final_w, til, final)
    return stream
```

```python
import functools
import math

import jax
import jax.numpy as jnp
import numpy as np
from jax import lax
from jax.experimental import pallas as pl
from jax.experimental.pallas import tpu as pltpu

F32 = jnp.float32
BF16 = jnp.bfloat16

D = 1024
DEPTH = 4
CTX = 256
GRID_W = 64
EPS = 1e-6
GN_EPS = 1e-5
ROPE_BASE = 10000.0

NH = 4
RET_DK, RET_DV = 32, 64
GLA_DK, GLA_DV = 32, 64
GLA_RANK = 16
GLA_TAU = 16.0
DIFF_HD, DIFF_DV = 64, 128
D_FF = 2816

O_RQ, O_RK, O_RV, O_RG = 0, 128, 256, 512
O_GQ, O_GK, O_GV, O_GR = 768, 896, 1024, 1280
O_LR = 1536
N_PROJ_A = O_LR + 2 * GLA_RANK

LANE = 128
PROJ_SUBTILES = 3
POST_TILING = {"first": (256, 3, 0), "middle": (288, 2, 0), "last": (256, 2, 1)}
TQ = 256
N_CAST = 3
RET_C = 128
GLA_C = 128
CHUNK_GROUP = 18
VMEM_LIMIT = 56 * 1024 * 1024


def _mm(a, b):
    return jnp.dot(a, b, preferred_element_type=F32)


def _mm_nt(a, b):
    return lax.dot_general(a, b, (((1,), (1,)), ((), ())), preferred_element_type=F32)


def _mm_tn(a, b):
    return lax.dot_general(a, b, (((0,), (0,)), ((), ())), preferred_element_type=F32)


def _split(x):
    hi = x.astype(BF16)
    lo = (x - hi.astype(F32)).astype(BF16)
    return hi, lo


def _hilo_mm(m_bf16, x):
    hi, lo = _split(x)
    n = x.shape[1]
    r = _mm(m_bf16, jnp.concatenate([hi, lo], axis=1))
    return r[:, :n] + r[:, n:]


def _group_mean(x, avg_bf16):
    hi, lo = _split(x)
    m = x.shape[0]
    r = _mm(jnp.concatenate([hi, lo], axis=0), avg_bf16)
    return r[:m] + r[m:]


def _log_sigmoid(z):
    return jnp.minimum(z, 0.0) - jnp.log(1.0 + jnp.exp(-jnp.abs(z)))


def _silu(z):
    return z / (1.0 + jnp.exp(-z))


def _iota(shape, dim):
    return lax.broadcasted_iota(jnp.int32, shape, dim)


def _head_masks(rows, lanes, head_shift):
    head = _iota((rows, lanes), 1) >> head_shift
    return [head == h for h in range(NH)]


def _stack_heads(x, masks):
    x = x.astype(F32)
    return jnp.concatenate([jnp.where(m, x, 0.0).astype(BF16) for m in masks], axis=0)


def _group_avg_matrix(n, group_shift):
    r = _iota((n, n), 0) >> group_shift
    c = _iota((n, n), 1) >> group_shift
    return jnp.where(r == c, 1.0 / (1 << group_shift), 0.0).astype(BF16)


def _ada_kernel(cond_ref, w_ref, b_ref, o_ref):
    cond = _silu(cond_ref[...])
    w = w_ref[0]
    c_hi, c_lo = _split(cond)
    w_hi, w_lo = _split(w)
    hh = _mm(jnp.concatenate([c_hi, c_lo], axis=0), w_hi)
    o_ref[0] = hh[:16] + hh[16:] + _mm(c_hi, w_lo) + b_ref[0]


def _ada(cond, w_ada, b_ada):
    bn = 1536
    n = w_ada.shape[-1]
    return pl.pallas_call(
        _ada_kernel,
        grid=(DEPTH, n // bn),
        in_specs=[
            pl.BlockSpec((16, D), lambda l, j: (0, 0)),
            pl.BlockSpec((1, D, bn), lambda l, j: (l, 0, j)),
            pl.BlockSpec((1, 1, bn), lambda l, j: (l, 0, j)),
        ],
        out_specs=pl.BlockSpec((1, 16, bn), lambda l, j: (l, 0, j)),
        out_shape=jax.ShapeDtypeStruct((DEPTH, 16, n), F32),
        compiler_params=pltpu.CompilerParams(dimension_semantics=("arbitrary", "arbitrary")),
        name="ada",
    )(cond, w_ada, b_ada.reshape(DEPTH, 1, n))


def _mod_specs(layer, j, batch):
    lat = pl.BlockSpec((None, 1, D), lambda b, t: ((layer * 16 + b) * 6 + j, 0, 0))
    ctx = pl.BlockSpec((None, 1, D), lambda b, t: ((layer * 16 + batch) * 6 + j, 0, 0))
    return [lat, ctx]


class _Tiling:
    def __init__(self, sub, n_sub, s0=0):
        self.sub, self.n_sub, self.s0 = sub, n_sub, s0
        self.rows = sub * n_sub
        self.spans = [(j * sub, (j + 1) * sub) for j in range(n_sub)]

    def s(self, j, t=None):
        return (pl.program_id(1) if t is None else t) * self.n_sub + j + self.s0

    def in_specs(self, arr, width):
        specs, ops = [], []
        for j in range(self.n_sub):
            if isinstance(arr, tuple):
                assert CTX % self.sub == 0
                n_ctx = CTX // self.sub
                specs += [
                    pl.BlockSpec((1, self.sub, width), lambda b, t, j=j: (b, jnp.minimum(self.s(j, t), n_ctx - 1), 0)),
                    pl.BlockSpec((1, self.sub, width), lambda b, t, j=j: (b, jnp.maximum(self.s(j, t) - n_ctx, 0), 0))]
                ops += list(arr)
            else:
                specs.append(pl.BlockSpec((1, self.sub, width), lambda b, t, j=j: (b, self.s(j, t), 0)))
                ops.append(arr)
        return specs, ops

    def load(self, refs, j):
        if len(refs) == 2 * self.n_sub:
            return jnp.where(self.s(j) < CTX // self.sub, refs[2 * j][0], refs[2 * j + 1][0])
        return refs[j][0]

    def mod(self, lat_ref, ctx_ref, j):
        if CTX % self.sub == 0:
            return jnp.where(self.s(j) < CTX // self.sub, ctx_ref[...], lat_ref[...])
        assert self.sub > CTX and self.s0 == 0
        if j > 0:
            return lat_ref[...]
        ctx_rows = (_iota((self.sub, 1), 0) < CTX) & (pl.program_id(1) == 0)
        return jnp.where(ctx_rows, ctx_ref[...], lat_ref[...])


def _rope(x, cos, sa, sb):
    outs = []
    for j in range(x.shape[1] // LANE):
        blk = x[:, j * LANE:(j + 1) * LANE]
        fwd = pltpu.roll(blk, LANE - 16, 1)
        bwd = pltpu.roll(blk, 16, 1)
        outs.append(blk * cos + fwd * sa + bwd * sb)
    return outs[0] if len(outs) == 1 else jnp.concatenate(outs, axis=1)


def _proj_kernel(*refs, n_stream, til):
    (sh_l, sh_c, sc_l, sc_c, nw_ref, w_ref, rcos, rsa, rsb, dcos, dsa, dsb,
     rq_o, rk_o, rv_o, rg_o, gq_o, gk_o, gv_o, gr_o, dq_o, dk_o, dv_o, lr_o, wa_ref, wd_ref) = refs[n_stream:]

    @pl.when((pl.program_id(0) == 0) & (pl.program_id(1) == 0))
    def _():
        wa_ref[...] = w_ref[0, :, 0:N_PROJ_A]
        wd_ref[...] = w_ref[0, :, N_PROJ_A:]

    wa, wd = wa_ref[...], wd_ref[...]
    h = []
    for j in range(til.n_sub):
        xj = til.load(refs[:n_stream], j)
        y = xj * lax.rsqrt(jnp.mean(xj * xj, axis=-1, keepdims=True) + EPS) * nw_ref[...]
        h.append((y * (1.0 + til.mod(sc_l, sc_c, j)) + til.mod(sh_l, sh_c, j)).astype(BF16))
    p = [(_mm(hj, wa), _mm(hj, wd)) for hj in h]

    avg = _group_avg_matrix(NH * RET_DV, 6)
    for (r0, r1), (pa, pd) in zip(til.spans, p):
        rc, ra, rb = rcos[r0:r1, :], rsa[r0:r1, :], rsb[r0:r1, :]
        rq_o[0, r0:r1, :] = _rope(pa[:, O_RQ:O_RQ + 128], rc, ra, rb).astype(BF16)
        rk_o[0, r0:r1, :] = (_rope(pa[:, O_RK:O_RK + 128], rc, ra, rb) * (RET_DK ** -0.5)).astype(BF16)
        rv = pa[:, O_RV:O_RV + 256]
        rv_o[0, r0:r1, :] = (rv - _group_mean(rv, avg)).astype(BF16)
        rg_o[0, r0:r1, :] = pa[:, O_RG:O_RG + 256]
        gq_o[0, r0:r1, :] = (pa[:, O_GQ:O_GQ + 128] * (GLA_DK ** -0.5)).astype(BF16)
        gk_o[0, r0:r1, :] = pa[:, O_GK:O_GK + 128].astype(BF16)
        gv_o[0, r0:r1, :] = pa[:, O_GV:O_GV + 256].astype(BF16)
        gr_o[0, r0:r1, :] = pa[:, O_GR:O_GR + 256]
        lr_o[0, r0:r1, :] = pa[:, O_LR:O_LR + 2 * GLA_RANK].astype(BF16)
        dc, da, db = dcos[r0:r1, :], dsa[r0:r1, :], dsb[r0:r1, :]
        dq_o[0, r0:r1, :] = (_rope(pd[:, 0:512], dc, da, db)
                             * (DIFF_HD ** -0.5 * math.log2(math.e))).astype(BF16)
        dk_o[0, r0:r1, :] = _rope(pd[:, 512:1024], dc, da, db).astype(BF16)
        dv_o[0, r0:r1, :] = pd[:, 1024:1536].astype(BF16)


def _proj(layer, batch, t_all, stream, mods, norm1_w, w_in_b, tabs):
    til = _Tiling(CTX, PROJ_SUBTILES)
    tm = til.rows
    tok = lambda n: pl.BlockSpec((1, tm, n), lambda b, t: (b, t, 0))
    tab = pl.BlockSpec((tm, LANE), lambda b, t: (t, 0))
    widths = [128, 128, 256, 256, 128, 128, 256, 256, 512, 512, 512, 2 * GLA_RANK]
    dtypes = [BF16, BF16, BF16, F32, BF16, BF16, BF16, F32, BF16, BF16, BF16, BF16]
    stream_specs, stream_ops = til.in_specs(stream, D)
    n_proj = w_in_b.shape[-1]
    return pl.pallas_call(
        functools.partial(_proj_kernel, n_stream=len(stream_ops), til=til),
        grid=(batch, t_all // tm),
        in_specs=stream_specs + _mod_specs(layer, 0, batch) + _mod_specs(layer, 1, batch) + [
            pl.BlockSpec((None, 1, D), lambda b, t: (layer, 0, 0)),
            pl.BlockSpec((1, D, n_proj), lambda b, t: (layer, 0, 0), pipeline_mode=pl.Buffered(1)),
            tab, tab, tab, tab, tab, tab,
        ],
        out_specs=[tok(n) for n in widths],
        out_shape=[jax.ShapeDtypeStruct((batch, t_all, n), dt) for n, dt in zip(widths, dtypes)],
        scratch_shapes=[pltpu.VMEM((D, N_PROJ_A), BF16), pltpu.VMEM((D, n_proj - N_PROJ_A), BF16)],
        compiler_params=pltpu.CompilerParams(
            dimension_semantics=("arbitrary", "arbitrary"), vmem_limit_bytes=VMEM_LIMIT),
        name="proj",
    )(*stream_ops, mods, mods, mods, mods, norm1_w, w_in_b, *tabs)


def _ret_phases(logit_ref, q_ref, k_ref, v_ref, g_ref, o_ref, ut, st):
    n_chunks = q_ref.shape[1] // RET_C
    n_ctx = CTX // RET_C
    c = RET_C

    lane_head = _iota((c, LANE), 1) >> 5
    i_f = _iota((c, LANE), 0).astype(F32)

    def lanes_of(d):
        z = jnp.zeros((c, LANE), F32)
        for h in range(NH):
            z = jnp.where(lane_head == h, logit_ref[d, h], z)
        return _log_sigmoid(z)

    lg_f, lg_b = lanes_of(0), lanes_of(1)
    kdec_f = jnp.exp((c - 1.0 - i_f) * lg_f)
    kdec_b = jnp.exp(i_f * lg_b)
    qdec_f = jnp.exp((i_f + 1.0) * lg_f)
    qdec_b = jnp.exp((c - i_f) * lg_b)
    cd_f = jnp.exp(c * lg_f[0:1])
    cd_b = jnp.exp(c * lg_b[0:1])
    kdec = jnp.concatenate([kdec_f, kdec_b], axis=1)
    qdec = jnp.concatenate([qdec_f, qdec_b], axis=1)

    nd = NH * RET_DV
    bd = (_iota((nd, 2 * LANE), 0) >> 6) == ((_iota((nd, 2 * LANE), 1) & (LANE - 1)) >> 5)

    def p0(i):
        for j in range(CHUNK_GROUP):
            n = i * CHUNK_GROUP + j
            rows = pl.ds(pl.multiple_of(n * c, c), c)
            k = k_ref[0, rows, :].astype(F32)
            k2 = (jnp.concatenate([k, k], axis=1) * kdec).astype(BF16)
            ut[n] = jnp.where(bd, _mm_tn(v_ref[0, rows, :], k2), 0.0)
        yield

    def scan():
        s = jnp.zeros((nd, LANE), F32)
        for n in range(n_chunks):
            st[n, :, 0:LANE] = s.astype(BF16)
            s = s * cd_f + ut[n, :, 0:LANE]
        s = jnp.zeros((nd, LANE), F32)
        for n in list(range(n_ctx - 1, -1, -1)) + list(range(n_chunks - 1, n_ctx - 1, -1)):
            st[n, :, LANE:] = s.astype(BF16)
            s = s * cd_b + ut[n, :, LANE:]

    rel = (_iota((c, c), 0) - _iota((c, c), 1)).astype(F32)
    eye = jnp.where(rel == 0.0, 1.0, 0.0)
    dmats = []
    for h in range(NH):
        lf = _log_sigmoid(jnp.full((c, c), logit_ref[0, h], F32))
        lb = _log_sigmoid(jnp.full((c, c), logit_ref[1, h], F32))
        dmats.append(jnp.exp(jnp.where(rel >= 0.0, rel * lf, -rel * lb)) + eye)
    dwide = jnp.concatenate(dmats, axis=1)
    kmasks, vmasks = _head_masks(c, LANE, 5), _head_masks(c, NH * RET_DV, 6)
    avg = _group_avg_matrix(NH * RET_DV, 6)

    def p2(i):
        ns = [i * CHUNK_GROUP + j for j in range(CHUNK_GROUP)]
        rows = [pl.ds(pl.multiple_of(n * c, c), c) for n in ns]
        sc, inter = [], []
        for n, r in zip(ns, rows):
            q = q_ref[0, r, :]
            sc.append(_mm_nt(q, _stack_heads(k_ref[0, r, :], kmasks)))
            qf = q.astype(F32)
            inter.append(_mm_nt((jnp.concatenate([qf, qf], axis=1) * qdec).astype(BF16), st[n]))
        yield
        o = []
        for r, s_c, o_c in zip(rows, sc, inter):
            o.append(_mm((s_c * dwide).astype(BF16), _stack_heads(v_ref[0, r, :], vmasks)) + o_c)
        yield
        var = [_mm((o_c * o_c).astype(BF16), avg) for o_c in o]
        yield
        for r, o_c, v_c in zip(rows, o, var):
            o_ref[0, r, :] = (_silu(g_ref[0, r, :]) * o_c * lax.rsqrt(v_c + GN_EPS)).astype(BF16)
        yield

    return p0, scan, p2


def _gla_phases(q_ref, k_ref, v_ref, lr_ref, g_ref, wgf_ref, wgb_ref, bg_ref, nw_ref, o_ref,
                cf_s, cb_s, ut, dec, st):
    c = GLA_C
    n_chunks = q_ref.shape[1] // c
    n_ctx = CTX // c
    nd = NH * GLA_DV

    ltri = jnp.where(_iota((c, c), 0) >= _iota((c, c), 1), 1.0, 0.0)
    utri = jnp.where(_iota((c, c), 0) <= _iota((c, c), 1), 1.0, 0.0)
    ltri_b, utri_b = ltri.astype(BF16), utri.astype(BF16)
    bd = (_iota((nd, 2 * LANE), 0) >> 6) == ((_iota((nd, 2 * LANE), 1) & (LANE - 1)) >> 5)

    lr = lr_ref[0]
    cf_s[...] = _log_sigmoid(_mm(lr, wgf_ref[...]) + bg_ref[0:1, :]) * (1.0 / GLA_TAU)
    cb_s[...] = _log_sigmoid(_mm(lr, wgb_ref[...]) + bg_ref[1:2, :]) * (1.0 / GLA_TAU)

    def p0(i):
        ns = [i * CHUNK_GROUP + j for j in range(CHUNK_GROUP)]
        rows = [pl.ds(pl.multiple_of(n * c, c), c) for n in ns]
        cf = [_hilo_mm(ltri_b, cf_s[r, :]) for r in rows]
        cb = [_hilo_mm(utri_b, cb_s[r, :]) for r in rows]
        yield
        for n, r, cf_c, cb_c in zip(ns, rows, cf, cb):
            cf_s[r, :] = cf_c
            cb_s[r, :] = cb_c
            cf_end, cb_end = cf_c[c - 1:c, :], cb_c[0:1, :]
            k = k_ref[0, r, :].astype(F32)
            k2 = jnp.concatenate([k * jnp.exp(cf_end - cf_c), k * jnp.exp(cb_end - cb_c)], axis=1).astype(BF16)
            ut[n] = jnp.where(bd, _mm_tn(v_ref[0, r, :], k2), 0.0)
            dec[n] = jnp.broadcast_to(jnp.exp(jnp.concatenate([cf_end, cb_end], axis=1)), (8, 2 * LANE))
        yield

    def scan():
        def scan_f(n, s):
            st[n, :, 0:LANE] = s.astype(BF16)
            return s * dec[n, 0:1, 0:LANE] + ut[n, :, 0:LANE]

        def scan_b(i, s, hi):
            n = hi - 1 - i
            st[n, :, LANE:] = s.astype(BF16)
            return s * dec[n, 0:1, LANE:] + ut[n, :, LANE:]

        zero = jnp.zeros((nd, LANE), F32)
        lax.fori_loop(0, n_chunks, scan_f, zero)
        s_ctx = lax.fori_loop(0, n_ctx, functools.partial(scan_b, hi=n_ctx), zero)
        lax.fori_loop(0, n_chunks - n_ctx, functools.partial(scan_b, hi=n_chunks), s_ctx)

    avg = _group_avg_matrix(nd, 6)
    nw = nw_ref[...]
    ltri_w = jnp.concatenate([ltri] * NH, axis=1)
    utri_w = jnp.concatenate([utri] * NH, axis=1)

    kmasks, vmasks = _head_masks(c, LANE, 5), _head_masks(c, nd, 6)

    def p2(i):
        ns = [i * CHUNK_GROUP + j for j in range(CHUNK_GROUP)]
        rows = [pl.ds(pl.multiple_of(n * c, c), c) for n in ns]
        s_fb, inter = [], []
        for n, r in zip(ns, rows):
            q = q_ref[0, r, :].astype(F32)
            k = k_ref[0, r, :].astype(F32)
            cf, cb = cf_s[r, :], cb_s[r, :]
            rf, rb = cf[c // 2 - 1:c // 2, :], cb[c // 2:c // 2 + 1, :]
            qf, qb = (q * jnp.exp(cf - rf)).astype(BF16), (q * jnp.exp(cb - rb)).astype(BF16)
            zq = jnp.zeros_like(qf)
            q_fb = jnp.concatenate([jnp.concatenate([qf, zq], axis=1), jnp.concatenate([zq, qb], axis=1)], axis=0)
            k_fb = jnp.concatenate([_stack_heads(k * jnp.exp(rf - cf), kmasks),
                                    _stack_heads(k * jnp.exp(rb - cb), kmasks)], axis=1)
            s_fb.append(_mm_nt(q_fb, k_fb))
            q2 = jnp.concatenate([q * jnp.exp(cf), q * jnp.exp(cb)], axis=1).astype(BF16)
            inter.append(_mm_nt(q2, st[n]))
        yield
        o = []
        for r, s_c, o_c in zip(rows, s_fb, inter):
            a = (s_c[:c] * ltri_w + s_c[c:] * utri_w).astype(BF16)
            o.append(_mm(a, _stack_heads(v_ref[0, r, :], vmasks)) + o_c)
        yield
        ms = [_mm((o_c * o_c).astype(BF16), avg) for o_c in o]
        yield
        for r, o_c, ms_c in zip(rows, o, ms):
            y = o_c * lax.rsqrt(ms_c + EPS) * nw
            o_ref[0, r, :] = (_silu(g_ref[0, r, :]) * y).astype(BF16)
        yield

    return p0, scan, p2


def _mix_kernel(logit_ref, rq, rk, rv, rg, gq, gk, gv, lr, gg, wgf, wgb, bg, nw, ro, go,
                r_ut, r_st, cf_s, cb_s, g_ut, g_dec, g_st):
    mixers = [_ret_phases(logit_ref, rq, rk, rv, rg, ro, r_ut, r_st),
              _gla_phases(gq, gk, gv, lr, gg, wgf, wgb, bg, nw, go, cf_s, cb_s, g_ut, g_dec, g_st)]
    n_groups = rq.shape[1] // (RET_C * CHUNK_GROUP)

    def lockstep(phase, i):
        done = object()
        gens = [m[phase](i) for m in mixers]
        while gens:
            gens = [g for g in gens if next(g, done) is not done]

    def p0(i, carry):
        lockstep(0, i)
        return carry

    def p2(i, carry):
        lockstep(2, i)
        return carry

    lax.fori_loop(0, n_groups, p0, 0)
    for m in mixers:
        m[1]()
    lax.fori_loop(0, n_groups, p2, 0)


def _mix(layer, rq, rk, rv, rg, gq, gk, gv, lr, gg, ret_decay_logit, wgf, wgb, bg, nw):
    batch, t_all, _ = rq.shape
    assert RET_C == GLA_C and t_all % (RET_C * CHUNK_GROUP) == 0
    n_chunks = t_all // RET_C
    nd = NH * GLA_DV
    tok = lambda n: pl.BlockSpec((1, t_all, n), lambda b, lg: (b, 0, 0))
    par = lambda r, n: pl.BlockSpec((None, r, n), lambda b, lg: (layer, 0, 0))
    state = lambda dt: pltpu.VMEM((n_chunks, nd, 2 * LANE), dt)
    return pl.pallas_call(
        _mix_kernel,
        grid_spec=pltpu.PrefetchScalarGridSpec(
            num_scalar_prefetch=1,
            grid=(batch,),
            in_specs=[tok(128), tok(128), tok(256), tok(256),
                      tok(128), tok(128), tok(256), tok(2 * GLA_RANK), tok(256),
                      par(2 * GLA_RANK, 128), par(2 * GLA_RANK, 128), par(2, 128), par(1, 256)],
            out_specs=[tok(256), tok(256)],
            scratch_shapes=[
                state(F32), state(BF16),
                pltpu.VMEM((t_all, LANE), F32), pltpu.VMEM((t_all, LANE), F32),
                state(F32), pltpu.VMEM((n_chunks, 8, 2 * LANE), F32), state(BF16),
            ],
        ),
        out_shape=[jax.ShapeDtypeStruct((batch, t_all, 256), BF16)] * 2,
        compiler_params=pltpu.CompilerParams(
            dimension_semantics=("arbitrary",), vmem_limit_bytes=VMEM_LIMIT),
        name="mix",
    )(ret_decay_logit[layer], rq, rk, rv, rg, gq, gk, gv, lr, gg, wgf, wgb, bg, nw)


def _attn_kernel(q_ref, k_ref, v_ref, lam_ref, nw_ref, *refs, lam_init):
    w_f32, (o_ref, *w_bf16), vx_ref = refs[:N_CAST], refs[N_CAST:2 * N_CAST + 1], refs[-1]
    for src, dst in zip(w_f32, w_bf16):
        dst[...] = src[0].astype(BF16)
    t_all = k_ref.shape[1]
    vx_ref[:, 0:DIFF_DV] = v_ref[0]
    vx_ref[:, DIFF_DV:] = jnp.ones((t_all, DIFF_DV), BF16)
    lp = lam_ref[...]
    lam = (jnp.exp(jnp.sum(lp[0:1] * lp[1:2], axis=-1, keepdims=True))
           - jnp.exp(jnp.sum(lp[2:3] * lp[3:4], axis=-1, keepdims=True)) + lam_init)
    first = _iota((TQ, LANE), 1) < DIFF_HD

    assert CTX == TQ
    tiles = [(r, CTX if r == 0 else t_all) for r in range(0, t_all, TQ)]
    s = []
    for r, n_keys in tiles:
        q = q_ref[0, r:r + TQ, :]
        for qh in (jnp.where(first, q, jnp.zeros_like(q)), jnp.where(first, jnp.zeros_like(q), q)):
            s.append(_mm_nt(qh, k_ref[0, 0:n_keys, :]).astype(BF16))
    e = [jnp.exp2(s_h - jnp.max(s_h, axis=-1, keepdims=True)) for s_h in s]
    ol = [_mm(e_h, vx_ref[0:e_h.shape[1], :]) for e_h in e]
    for j, (r, _) in enumerate(tiles):
        o1, o2 = (x[:, 0:DIFF_DV] / x[:, DIFF_DV:] for x in ol[2 * j:2 * j + 2])
        o = o1 - lam * o2
        y = o * lax.rsqrt(jnp.mean(o * o, axis=-1, keepdims=True) + EPS) * nw_ref[...]
        o_ref[0, r:r + TQ, :] = (y * (1.0 - lam_init)).astype(BF16)


def _attn(layer, dq, dk, dv, diff_lambda, diff_norm_w, weights_f32):
    batch, t_all, _ = dq.shape
    assert len(weights_f32) == N_CAST
    lam_init = 0.8 - 0.6 * math.exp(-0.3 * layer)
    head = pl.BlockSpec((1, t_all, LANE), lambda b, h: (b, 0, h))

    def slab(w):
        n_steps = max(n for n in range(1, batch * NH + 1) if w.shape[1] % (16 * n) == 0)
        return w.shape[1] // n_steps, lambda b, h: jnp.minimum(b * NH + h, n_steps - 1)

    slabs = [slab(w) for w in weights_f32]
    out = pl.pallas_call(
        functools.partial(_attn_kernel, lam_init=lam_init),
        grid=(batch, NH),
        in_specs=[
            head, head, head,
            pl.BlockSpec((None, 4, DIFF_HD), lambda b, h: (layer, 0, 0)),
            pl.BlockSpec((None, 1, DIFF_DV), lambda b, h: (layer, 0, 0)),
        ] + [pl.BlockSpec((1, r, w.shape[-1]), lambda b, h, i=i: (layer, i(b, h), 0))
             for w, (r, i) in zip(weights_f32, slabs)],
        out_specs=[head] + [pl.BlockSpec((r, w.shape[-1]), lambda b, h, i=i: (i(b, h), 0))
                            for w, (r, i) in zip(weights_f32, slabs)],
        out_shape=[jax.ShapeDtypeStruct((batch, t_all, NH * DIFF_DV), BF16)]
        + [jax.ShapeDtypeStruct(w.shape[1:], BF16) for w in weights_f32],
        scratch_shapes=[pltpu.VMEM((t_all, 2 * DIFF_DV), BF16)],
        compiler_params=pltpu.CompilerParams(
            dimension_semantics=("arbitrary", "arbitrary"), vmem_limit_bytes=VMEM_LIMIT),
        name="attn",
    )(dq, dk, dv, diff_lambda, diff_norm_w, *weights_f32)
    return out[0], out[1:]


def _post_kernel(*refs, n_stream, final, til):
    n = til.n_sub
    x_refs = refs[:n_stream]
    ro_refs, go_refs, do_refs = (refs[n_stream + i * n:n_stream + (i + 1) * n] for i in range(3))
    (g1_l, g1_c, sh_l, sh_c, sc_l, sc_c, g2_l, g2_c,
     wo_ref, nw_ref, wi_ref, wf_ref, fw_ref, o_ref) = refs[n_stream + 3 * n:]

    def rms(v, w_ref):
        return v * lax.rsqrt(jnp.mean(v * v, axis=-1, keepdims=True) + EPS) * w_ref[...]

    subs = range(n)
    att = [_mm(jnp.concatenate([ro_refs[j][0], go_refs[j][0], do_refs[j][0]], axis=-1), wo_ref[0]) for j in subs]
    xm = [til.load(x_refs, j) + til.mod(g1_l, g1_c, j) * att[j] for j in subs]
    h = [(rms(xm[j], nw_ref) * (1.0 + til.mod(sc_l, sc_c, j)) + til.mod(sh_l, sh_c, j)).astype(BF16) for j in subs]
    gu = [_mm(h[j], wi_ref[0]) for j in subs]
    ff = [_mm((_silu(g[:, :D_FF]) * g[:, D_FF:]).astype(BF16), wf_ref[0]) for g in gu]
    for j, (r0, r1) in enumerate(til.spans):
        xo = xm[j] + til.mod(g2_l, g2_c, j) * ff[j]
        o_ref[0, r0:r1, :] = rms(xo, fw_ref) if final else xo


def _post(layer, batch, t_all, stream, ro, go, do, mods, w_out, norm2_w, w_ffn_in, w_ffn_out, final_w, til, final):
    n_rows = t_all - til.s0 * til.sub
    wspec = lambda r, n: pl.BlockSpec((1, r, n), lambda b, t: (0, 0, 0), pipeline_mode=pl.Buffered(1))
    stream_specs, stream_ops = til.in_specs(stream, D)
    tok_specs, tok_ops = zip(*(til.in_specs(a, a.shape[-1]) for a in (ro, go, do)))
    mod_specs = sum((_mod_specs(layer, j, batch) for j in (2, 3, 4, 5)), [])
    return pl.pallas_call(
        functools.partial(_post_kernel, n_stream=len(stream_ops), final=final, til=til),
        grid=(batch, n_rows // til.rows),
        in_specs=stream_specs + sum(tok_specs, []) + mod_specs + [
            wspec(D, D),
            pl.BlockSpec((None, 1, D), lambda b, t: (layer, 0, 0)),
            wspec(D, 2 * D_FF), wspec(D_FF, D),
            pl.BlockSpec((1, D), lambda b, t: (0, 0)),
        ],
        out_specs=pl.BlockSpec((1, til.rows, D), lambda b, t: (b, t, 0)),
        out_shape=jax.ShapeDtypeStruct((batch, n_rows, D), F32),
        compiler_params=pltpu.CompilerParams(
            dimension_semantics=("arbitrary", "arbitrary"), vmem_limit_bytes=VMEM_LIMIT),
        name="post",
    )(*stream_ops, *sum(tok_ops, []), *([mods] * 8), w_out, norm2_w, w_ffn_in, w_ffn_out, final_w)


def _rope_tables(n_lat):
    lane = np.arange(LANE)
    first = ((lane % 32) < 16)[None, :]

    def pack(ang):
        cos, sin = np.cos(ang), np.sin(ang)
        tabs = (cos, np.where(first, -sin, 0.0), np.where(first, 0.0, sin))
        ident = (np.ones((CTX, LANE)), np.zeros((CTX, LANE)), np.zeros((CTX, LANE)))
        return tuple(jnp.asarray(np.concatenate([i, t], axis=0), dtype=F32) for i, t in zip(ident, tabs))

    idx = np.arange(n_lat, dtype=np.float64)
    ret_freq = 1.0 / (ROPE_BASE ** np.linspace(0.0, 1.0, RET_DK // 2))
    ret = pack((idx[:, None] * ret_freq[None, :])[:, lane % 16])
    ax_freq = 1.0 / (ROPE_BASE ** (np.arange(DIFF_HD // 4) / (DIFF_HD // 4)))
    row_ang = np.floor(idx / GRID_W)[:, None] * ax_freq[None, :]
    col_ang = (idx % GRID_W)[:, None] * ax_freq[None, :]
    is_row = ((lane % 64) < 32)[None, :]
    diff = pack(np.where(is_row, row_ang[:, lane % 16], col_ang[:, lane % 16]))
    return ret + diff


def kernel(x, c, ctx, c_ctx, w_ada, b_ada, norm1_w, w_in, ret_decay_logit, gla_w_gate, gla_b_gate,
           gla_norm_w, diff_lambda, diff_norm_w, w_out, norm2_w, w_ffn_in, w_ffn_out, final_norm_w):
    batch, n_lat, _ = x.shape
    assert ctx.shape[1] == CTX and batch < 16

    w_in_b = w_in.astype(BF16)
    zpad = jnp.zeros((DEPTH, GLA_RANK, NH * GLA_DK), F32)
    wgf = jnp.concatenate([gla_w_gate[:, 0], zpad], axis=1).astype(BF16)
    wgb = jnp.concatenate([zpad, gla_w_gate[:, 1]], axis=1).astype(BF16)
    gla_nw = jnp.tile(gla_norm_w, (1, NH)).reshape(DEPTH, 1, NH * GLA_DV)
    norm1 = norm1_w.reshape(DEPTH, 1, D)
    norm2 = norm2_w.reshape(DEPTH, 1, D)
    diff_nw = diff_norm_w.reshape(DEPTH, 1, DIFF_DV)
    final_w = final_norm_w.reshape(1, D)
    tabs = _rope_tables(n_lat)

    cond = jnp.concatenate([c, c_ctx[None, :], jnp.zeros((16 - batch - 1, D), F32)], axis=0)
    mods = _ada(cond, w_ada, b_ada).reshape(DEPTH * 16 * 6, 1, D)

    t_all = CTX + n_lat
    stream = (ctx, x)
    for layer in range(DEPTH):
        final = layer == DEPTH - 1
        til = _Tiling(*POST_TILING["first" if layer == 0 else "last" if final else "middle"])
        rq, rk, rv, rg, gq, gk, gv, gr, dq, dk, dv, lr = _proj(
            layer, batch, t_all, stream, mods, norm1, w_in_b, tabs)
        ro, go = _mix(layer, rq, rk, rv, rg, gq, gk, gv, lr, gr, ret_decay_logit, wgf, wgb, gla_b_gate, gla_nw)
        do, (wo_b, wi_b, wf_b) = _attn(layer, dq, dk, dv, diff_lambda, diff_nw, (w_out, w_ffn_in, w_ffn_out))
        stream = _post(layer, batch, t_all, stream, ro, go, do, mods, wo_b[None], norm2, wi_b[None], wf_b[None],
                       final_w, til, final)
    return stream
```
